```python
import math
import jax
import jax.numpy as jnp
from jax import lax
import numpy as np

D_MODEL = 1024
BATCH = 4
SEQ = 8192
DEPTH = 2
DEC_BATCH = 32
DEC_SEQ = 8
PAST_LEN = 16384
PAGE_SIZE = 128

HEAD_DIM = 64
CONV_CH = 256
CONV_WIDTH = 31
FOX_HEADS = 6
NSA_HEADS = 6
NSA_KV_HEADS = 2
NSA_REP = NSA_HEADS // NSA_KV_HEADS
CMP_BLOCK = 32
CMP_STRIDE = 16
SEL_BLOCK = 64
SEL_TOPK = 16
SEL_PER_CMP = SEL_BLOCK // CMP_STRIDE
WINDOW = 512
N_BUCKETS = 32
MAX_DISTANCE = 128
D_FF = 2816
FFN_CONV_WIDTH = 3
Q_BLOCK = 128
LN_EPS = 1e-5
ALPHA = (2 * DEPTH) ** 0.25
BETA = (8 * DEPTH) ** -0.25

FOX_W = FOX_HEADS * HEAD_DIM
NSA_W = NSA_HEADS * HEAD_DIM
NSA_KV_W = NSA_KV_HEADS * HEAD_DIM
MIX_W = CONV_CH + FOX_W + NSA_W
SEG_SIZES = (2 * CONV_CH, FOX_W, FOX_W, FOX_W, FOX_HEADS, NSA_W, 6 * NSA_KV_W, 3 * NSA_HEADS)
IN_COLS = sum(SEG_SIZES)
SEG_SPLITS = tuple(int(v) for v in np.cumsum(SEG_SIZES)[:-1])
ROW_KEYS = ('fox_k', 'fox_v', 'fox_logf', 'nsa_cmp', 'nsa_sel')
STATE_KEYS = ('nsa_win', 'conv', 'ffn_conv')

kernel_name = 'hymba_conformer_fox_nsa_decoder_step'


def layer_norm(x, g, b):
    xf = x.astype(jnp.float32)
    mu = jnp.mean(xf, axis=-1, keepdims=True)
    var = jnp.mean(jnp.square(xf - mu), axis=-1, keepdims=True)
    y = (xf - mu) * lax.rsqrt(var + LN_EPS) * g.astype(jnp.float32) + b.astype(jnp.float32)
    return y.astype(x.dtype)


def masked_softmax(s, mask):
    s = jnp.where(mask, s.astype(jnp.float32), -jnp.inf)
    m = jnp.max(s, axis=-1, keepdims=True)
    m = jnp.where(jnp.isfinite(m), m, 0.0)
    e = jnp.exp(s - m)
    return e / jnp.maximum(jnp.sum(e, axis=-1, keepdims=True), 1e-30)


def causal_dwconv(x, buf, w):
    xp = jnp.concatenate([buf.astype(x.dtype), x], axis=1)
    y = lax.conv_general_dilated(xp, w[:, None, :].astype(x.dtype), (1,), 'VALID',
                                 dimension_numbers=('NWC', 'WIO', 'NWC'),
                                 feature_group_count=x.shape[-1])
    return y, xp[:, xp.shape[1] - (w.shape[0] - 1):]


def rel_bucket(dist):
    max_exact = N_BUCKETS // 2
    d = jnp.maximum(dist, 0)
    df = jnp.maximum(d, 1).astype(jnp.float32)
    log_b = max_exact + (jnp.log(df / max_exact) / math.log(MAX_DISTANCE / max_exact)
                         * (N_BUCKETS - max_exact)).astype(jnp.int32)
    return jnp.where(d < max_exact, d, jnp.minimum(log_b, N_BUCKETS - 1))


def over_query_blocks(fn, *arrays):
    B, Tq = arrays[0].shape[:2]
    qb = min(Q_BLOCK, Tq)
    nb = -(-Tq // qb)
    pad = nb * qb - Tq

    def split(a):
        a = jnp.pad(a, [(0, 0), (0, pad)] + [(0, 0)] * (a.ndim - 2))
        return jnp.moveaxis(a.reshape((B, nb, qb) + a.shape[2:]), 1, 0)

    pos = jnp.arange(nb * qb, dtype=jnp.int32).reshape(nb, qb)
    out = lax.map(lambda args: fn(*args), (pos,) + tuple(split(a) for a in arrays))
    out = jnp.moveaxis(out, 0, 1)
    return out.reshape((B, nb * qb) + out.shape[3:])[:, :Tq]


def compress(rows, w):
    B, Tk, G, hd = rows.shape
    nch = -(-Tk // CMP_STRIDE)
    rows = jnp.pad(rows, ((0, 0), (0, nch * CMP_STRIDE - Tk), (0, 0), (0, 0)))
    ch = rows.reshape(B, nch, CMP_STRIDE, G, hd)
    lead = jnp.einsum('bcjgd,jgd->bcgd', ch, w[:CMP_STRIDE])
    tail = jnp.einsum('bcjgd,jgd->bcgd', ch, w[CMP_STRIDE:])
    return lead[:, :-1] + tail[:, 1:]


def fox_mixer(q, k, v, f_logit, b_f, past):
    logf = jax.nn.log_sigmoid(f_logit.astype(jnp.float32) + b_f.astype(jnp.float32))
    if past is None:
        k_all, v_all, logf_all, q_start = k, v, logf, 0
    else:
        k_past, v_past, logf_past = past
        q_start = k_past.shape[1]
        k_all = jnp.concatenate([k_past.astype(k.dtype), k], axis=1)
        v_all = jnp.concatenate([v_past.astype(v.dtype), v], axis=1)
        logf_all = jnp.concatenate([logf_past.astype(jnp.float32), logf], axis=1)
    c_t = jnp.swapaxes(jnp.cumsum(logf_all, axis=1), 1, 2)
    Tk = k_all.shape[1]
    kpos = jnp.arange(Tk, dtype=jnp.int32)
    scale = HEAD_DIM ** -0.5

    def block(pos, qb):
        t = q_start + pos
        tc = jnp.minimum(t, Tk - 1)
        s = jnp.einsum('bqhd,bkhd->bhqk', qb, k_all).astype(jnp.float32) * scale
        s = s + c_t[:, :, tc][..., None] - c_t[:, :, None, :]
        p = masked_softmax(s, kpos[None, :] <= t[:, None])
        return jnp.einsum('bhqk,bkhd->bqhd', p.astype(v_all.dtype), v_all)

    return over_query_blocks(block, q), logf


def nsa_mixer(q, kv6, gate_logit, cmp_w, rel_bias, past):
    B, T = q.shape[:2]
    G, R, hd = NSA_KV_HEADS, NSA_REP, HEAD_DIM
    cmp_new, sel_new, win_new = kv6[:, :, 0:2], kv6[:, :, 2:4], kv6[:, :, 4:6]
    if past is None:
        cmp_all, sel_all, win_all, q_start, w_start = cmp_new, sel_new, win_new, 0, 0
        buf_len = min(WINDOW, T)
    else:
        cmp_past, sel_past, win_past = past
        q_start, w_start = cmp_past.shape[1], win_past.shape[1]
        buf_len = w_start
        cmp_all = jnp.concatenate([cmp_past.astype(kv6.dtype), cmp_new], axis=1)
        sel_all = jnp.concatenate([sel_past.astype(kv6.dtype), sel_new], axis=1)
        win_all = jnp.concatenate([win_past.astype(kv6.dtype), win_new], axis=1)
    win_state = win_all[:, win_all.shape[1] - buf_len:]
    Tk = cmp_all.shape[1]
    scale = hd ** -0.5

    k_c = compress(cmp_all[:, :, 0], cmp_w[0])
    v_c = compress(cmp_all[:, :, 1], cmp_w[1])
    NC = k_c.shape[1]
    c_end = jnp.arange(NC, dtype=jnp.int32) * CMP_STRIDE + (CMP_BLOCK - 1)

    NS = -(-Tk // SEL_BLOCK)
    def blockify(a):
        a = jnp.pad(a, ((0, 0), (0, NS * SEL_BLOCK - Tk), (0, 0), (0, 0)))
        return a.reshape(B, NS, SEL_BLOCK, G, hd).transpose(0, 3, 1, 2, 4)
    k_s, v_s = blockify(sel_all[:, :, 0]), blockify(sel_all[:, :, 1])
    topk = min(SEL_TOPK, NS)
    gather = jax.vmap(jax.vmap(lambda blk, ix: blk[ix]))
    bias_g = rel_bias.reshape(N_BUCKETS, G, R).transpose(1, 0, 2)
    gidx = jnp.arange(G)[None, :, None, None]

    qbl = min(Q_BLOCK, T)
    KW = WINDOW + qbl
    pad_w = ((0, 0), (WINDOW, Q_BLOCK), (0, 0), (0, 0))
    k_w, v_w = jnp.pad(win_all[:, :, 0], pad_w), jnp.pad(win_all[:, :, 1], pad_w)

    def head_bias(dist):
        return rel_bias[rel_bucket(dist)].reshape(dist.shape + (G, R)).transpose(2, 3, 0, 1)

    def block(pos, qb, gb):
        QB = pos.shape[0]
        t = q_start + pos
        qg = qb.reshape(B, QB, G, R, hd)
        dist = t[:, None] - c_end[None, :]
        s = jnp.einsum('bqgrd,bngd->bgrqn', qg, k_c).astype(jnp.float32) * scale + head_bias(dist)
        p_c = masked_softmax(s, dist >= 0)
        o_c = jnp.einsum('bgrqn,bngd->bqgrd', p_c.astype(v_c.dtype), v_c)
        imp = jnp.sum(p_c, axis=2)
        imp = jnp.pad(imp, ((0, 0), (0, 0), (0, 0), (0, SEL_PER_CMP * NS - NC))).reshape(B, G, QB, NS, SEL_PER_CMP)
        imp = jnp.sum(imp, axis=-1) + jnp.pad(imp[..., :-1, -1], ((0, 0), (0, 0), (0, 0), (1, 0)))
        j = jnp.arange(NS, dtype=jnp.int32)[None, :]
        cur = (t // SEL_BLOCK)[:, None]
        forced = (j == 0) | (j == cur) | (j == cur - 1)
        score = jnp.where(forced, 1e9, jnp.where(j <= cur, imp, -1e9))
        _, idx = lax.top_k(score, topk)
        ks = gather(k_s, idx).reshape(B, G, QB, topk * SEL_BLOCK, hd)
        vs = gather(v_s, idx).reshape(B, G, QB, topk * SEL_BLOCK, hd)
        kpos = (idx[..., None] * SEL_BLOCK + jnp.arange(SEL_BLOCK, dtype=jnp.int32)).reshape(B, G, QB, topk * SEL_BLOCK)
        dist = t[None, None, :, None] - kpos
        s = jnp.einsum('bqgrd,bgqkd->bgrqk', qg, ks).astype(jnp.float32) * scale
        s = s + jnp.moveaxis(bias_g[gidx, rel_bucket(dist)], -1, 2)
        p_s = masked_softmax(s, (dist >= 0)[:, :, None])
        o_s = jnp.einsum('bgrqk,bgqkd->bqgrd', p_s.astype(vs.dtype), vs)
        lt = w_start + pos
        start = w_start + pos[0]
        kw = lax.dynamic_slice_in_dim(k_w, start, KW, axis=1)
        vw = lax.dynamic_slice_in_dim(v_w, start, KW, axis=1)
        kl = start - WINDOW + jnp.arange(KW, dtype=jnp.int32)
        dist = lt[:, None] - kl[None, :]
        mask = (kl[None, :] >= 0) & (dist >= 0) & (dist <= WINDOW)
        s = jnp.einsum('bqgrd,bkgd->bgrqk', qg, kw).astype(jnp.float32) * scale + head_bias(dist)
        p_w = masked_softmax(s, mask)
        o_w = jnp.einsum('bgrqk,bkgd->bqgrd', p_w.astype(vw.dtype), vw)
        g = jax.nn.sigmoid(gb.astype(jnp.float32)).reshape(B, QB, G, R, 3)
        o = g[..., 0:1] * o_c + g[..., 1:2] * o_s + g[..., 2:3] * o_w
        return o.reshape(B, QB, NSA_W).astype(qb.dtype)

    return over_query_blocks(block, q, gate_logit), win_state


def layer(x, l, p, past):
    B, T, _ = x.shape
    proj = jnp.einsum('btd,dc->btc', x, p['w_in'][l])
    glu, fq, fk, fv, ff, nq, nkv, ng = jnp.split(proj, SEG_SPLITS, axis=-1)
    heads = lambda a, h: a.reshape(B, T, h, HEAD_DIM)

    u = glu[..., :CONV_CH] * jax.nn.sigmoid(glu[..., CONV_CH:])
    conv_buf = jnp.zeros((B, CONV_WIDTH - 1, CONV_CH), u.dtype) if past is None else past['conv']
    u, conv_state = causal_dwconv(u, conv_buf, p['conv_dw'][l])
    u = jax.nn.silu(layer_norm(u, p['conv_ln_g'][l], p['conv_ln_b'][l]))
    out_a = jnp.einsum('btc,ce->bte', u, p['conv_pw'][l])

    fk_h, fv_h = heads(fk, FOX_HEADS), heads(fv, FOX_HEADS)
    out_b, logf = fox_mixer(heads(fq, FOX_HEADS), fk_h, fv_h, ff, p['b_fgate'][l],
                            None if past is None else past['fox'])

    kv6 = nkv.reshape(B, T, 6, NSA_KV_HEADS, HEAD_DIM)
    out_c, win_state = nsa_mixer(heads(nq, NSA_HEADS), kv6, ng.reshape(B, T, NSA_HEADS, 3),
                                 p['nsa_cmp_w'][l], p['rel_bias'],
                                 None if past is None else past['nsa'])

    mixed = jnp.concatenate([out_a, out_b.reshape(B, T, FOX_W), out_c], axis=-1)
    x = layer_norm(ALPHA * x + jnp.einsum('btc,cd->btd', mixed, p['w_out'][l]), p['ln1_g'][l], p['ln1_b'][l])

    up, gate = jnp.split(jnp.einsum('btd,df->btf', x, p['ffn_w_in'][l]), 2, axis=-1)
    ffn_buf = jnp.zeros((B, FFN_CONV_WIDTH - 1, D_FF), up.dtype) if past is None else past['ffn_conv']
    up, ffn_state = causal_dwconv(up, ffn_buf, p['ffn_dw'][l])
    ffn = jnp.einsum('btf,fd->btd', jax.nn.gelu(up) * gate, p['ffn_w_out'][l])
    x = layer_norm(ALPHA * x + ffn, p['ln2_g'][l], p['ln2_b'][l])

    new = dict(fox_k=fk_h, fox_v=fv_h, fox_logf=logf, nsa_cmp=kv6[:, :, 0:2], nsa_sel=kv6[:, :, 2:4],
               nsa_win=win_state, conv=conv_state, ffn_conv=ffn_state)
    return x, new


def gather_pages(cache, page_table, l):
    g = cache[page_table, l]
    return g.reshape((g.shape[0], g.shape[1] * g.shape[2]) + g.shape[3:])


def layer_past(cache, l):
    pt = cache['page_table']
    return dict(
        fox=(gather_pages(cache['fox_k'], pt, l), gather_pages(cache['fox_v'], pt, l),
             gather_pages(cache['fox_logf'], pt, l)),
        nsa=(gather_pages(cache['nsa_cmp'], pt, l), gather_pages(cache['nsa_sel'], pt, l), cache['nsa_win'][l]),
        conv=cache['conv'][l], ffn_conv=cache['ffn_conv'][l])


def trunk(x, p, cache):
    collected = {k: [] for k in ROW_KEYS + STATE_KEYS}
    for l in range(DEPTH):
        past = None if cache is None else layer_past(cache, l)
        x, new = layer(x, l, p, past)
        for k in collected:
            collected[k].append(new[k])
    out = {k: jnp.stack(collected[k], axis=1) for k in ROW_KEYS}
    out.update({k: jnp.stack(collected[k], axis=0) for k in STATE_KEYS})
    return x, out


def setup_inputs(seed: int = 0) -> dict:
    key = jax.random.key(seed)
    ks = jax.random.split(key, 32)
    f32 = jnp.float32
    n_pages = PAST_LEN // PAGE_SIZE
    n_used = DEC_BATCH * n_pages
    n_pool = n_used + n_used // 4
    win_len = min(WINDOW, PAST_LEN)
    nrm = lambda k, shape, s=1.0: s * jax.random.normal(k, shape, f32)
    page_table = jax.random.permutation(ks[0], n_pool)[:n_used].reshape(DEC_BATCH, n_pages).astype(jnp.int32)
    return {
        'x_prompt': nrm(ks[1], (BATCH, SEQ, D_MODEL)),
        'x_sample': nrm(ks[2], (DEC_BATCH, DEC_SEQ, D_MODEL)),
        'cache_fox_k': nrm(ks[3], (n_pool, DEPTH, PAGE_SIZE, FOX_HEADS, HEAD_DIM)),
        'cache_fox_v': nrm(ks[4], (n_pool, DEPTH, PAGE_SIZE, FOX_HEADS, HEAD_DIM)),
        'cache_fox_logf': jax.nn.log_sigmoid(3.0 + nrm(ks[5], (n_pool, DEPTH, PAGE_SIZE, FOX_HEADS))),
        'cache_nsa_cmp': nrm(ks[6], (n_pool, DEPTH, PAGE_SIZE, 2, NSA_KV_HEADS, HEAD_DIM)),
        'cache_nsa_sel': nrm(ks[7], (n_pool, DEPTH, PAGE_SIZE, 2, NSA_KV_HEADS, HEAD_DIM)),
        'state_nsa_win': nrm(ks[8], (DEPTH, DEC_BATCH, win_len, 2, NSA_KV_HEADS, HEAD_DIM)),
        'state_conv': nrm(ks[9], (DEPTH, DEC_BATCH, CONV_WIDTH - 1, CONV_CH), 0.5),
        'state_ffn_conv': nrm(ks[10], (DEPTH, DEC_BATCH, FFN_CONV_WIDTH - 1, D_FF)),
        'page_table': page_table,
        'w_in': nrm(ks[11], (DEPTH, D_MODEL, IN_COLS), D_MODEL ** -0.5),
        'b_fgate': jax.random.uniform(ks[12], (DEPTH, FOX_HEADS), f32, 1.0, 5.0),
        'conv_dw': nrm(ks[13], (DEPTH, CONV_WIDTH, CONV_CH), CONV_WIDTH ** -0.5),
        'conv_ln_g': 1.0 + nrm(ks[14], (DEPTH, CONV_CH), 0.05),
        'conv_ln_b': nrm(ks[15], (DEPTH, CONV_CH), 0.02),
        'conv_pw': nrm(ks[16], (DEPTH, CONV_CH, CONV_CH), CONV_CH ** -0.5),
        'nsa_cmp_w': (1.0 + nrm(ks[17], (DEPTH, 2, CMP_BLOCK, NSA_KV_HEADS, HEAD_DIM), 0.5)) * CMP_BLOCK ** -0.5,
        'rel_bias': nrm(ks[18], (N_BUCKETS, NSA_HEADS), 0.5),
        'w_out': nrm(ks[19], (DEPTH, MIX_W, D_MODEL), BETA * MIX_W ** -0.5),
        'ln1_g': 1.0 + nrm(ks[20], (DEPTH, D_MODEL), 0.05),
        'ln1_b': nrm(ks[21], (DEPTH, D_MODEL), 0.02),
        'ffn_w_in': nrm(ks[22], (DEPTH, D_MODEL, 2 * D_FF), D_MODEL ** -0.5),
        'ffn_dw': nrm(ks[23], (DEPTH, FFN_CONV_WIDTH, D_FF), FFN_CONV_WIDTH ** -0.5),
        'ffn_w_out': nrm(ks[24], (DEPTH, D_FF, D_MODEL), BETA * D_FF ** -0.5),
        'ln2_g': 1.0 + nrm(ks[25], (DEPTH, D_MODEL), 0.05),
        'ln2_b': nrm(ks[26], (DEPTH, D_MODEL), 0.02),
    }


def reference(x_prompt, x_sample, cache_fox_k, cache_fox_v, cache_fox_logf, cache_nsa_cmp, cache_nsa_sel,
              state_nsa_win, state_conv, state_ffn_conv, page_table, w_in, b_fgate, conv_dw, conv_ln_g,
              conv_ln_b, conv_pw, nsa_cmp_w, rel_bias, w_out, ln1_g, ln1_b, ffn_w_in, ffn_dw, ffn_w_out,
              ln2_g, ln2_b):
    p = dict(w_in=w_in, b_fgate=b_fgate, conv_dw=conv_dw, conv_ln_g=conv_ln_g, conv_ln_b=conv_ln_b,
             conv_pw=conv_pw, nsa_cmp_w=nsa_cmp_w, rel_bias=rel_bias, w_out=w_out, ln1_g=ln1_g, ln1_b=ln1_b,
             ffn_w_in=ffn_w_in, ffn_dw=ffn_dw, ffn_w_out=ffn_w_out, ln2_g=ln2_g, ln2_b=ln2_b)
    y_prompt, sp = trunk(x_prompt, p, None)
    cache = dict(fox_k=cache_fox_k, fox_v=cache_fox_v, fox_logf=cache_fox_logf, nsa_cmp=cache_nsa_cmp,
                 nsa_sel=cache_nsa_sel, nsa_win=state_nsa_win, conv=state_conv, ffn_conv=state_ffn_conv,
                 page_table=page_table)
    y_sample, ss = trunk(x_sample, p, cache)
    return (y_prompt, y_sample,
            sp['fox_k'], sp['fox_v'], sp['fox_logf'], sp['nsa_cmp'], sp['nsa_sel'], sp['nsa_win'], sp['conv'], sp['ffn_conv'],
            ss['fox_k'], ss['fox_v'], ss['fox_logf'], ss['nsa_cmp'], ss['nsa_sel'], ss['nsa_win'], ss['conv'], ss['ffn_conv'])
```

```python
import functools
import math

import numpy as np
import jax
import jax.numpy as jnp
from jax import lax
from jax.experimental import pallas as pl
from jax.experimental.pallas import tpu as pltpu

F32 = jnp.float32
BF16 = jnp.bfloat16

HEAD_DIM = 64
LANES = 128
CONV_CH = 256
CONV_WIDTH = 31
CONV_CARRY = 32
FOX_HEADS = 6
NSA_HEADS = 6
NSA_KV_HEADS = 2
NSA_REP = NSA_HEADS // NSA_KV_HEADS
CMP_BLOCK = 32
CMP_STRIDE = 16
SEL_BLOCK = 64
SEL_TOPK = 16
SEL_PER_CMP = SEL_BLOCK // CMP_STRIDE
WINDOW = 512
N_BUCKETS = 32
MAX_DISTANCE = 128
FFN_CONV_WIDTH = 3
FFN_CARRY = 8
LN_EPS = 1e-5
PAGE = 128
PAGES_PER_STEP = 8

FOX_W = FOX_HEADS * HEAD_DIM
NSA_W = NSA_HEADS * HEAD_DIM
NSA_KV_W = NSA_KV_HEADS * HEAD_DIM
WIDE = 6 * LANES

NEG = -1e30
NEG_TEST = -1e29
SEL_NEG = -32768.0

C_GLU, C_FQ, C_FK, C_FV, C_NQ, C_CMP, C_SEL, C_WIN, C_SM, C_END = (
    0, 512, 1280, 1664, 2048, 2816, 3072, 3328, 3584, 3712)
SM_FF = 0
SM_NG = 6


def _cparams(sem, vmem_mb=48):
    return pltpu.CompilerParams(dimension_semantics=sem, vmem_limit_bytes=vmem_mb * 1024 * 1024)


def _dot(a, b):
    return jnp.dot(a, b, preferred_element_type=F32)


def _nt(a, b):
    return lax.dot_general(a, b, (((1,), (1,)), ((), ())), preferred_element_type=F32)


def _dot3(x, w):
    hi = x.astype(BF16)
    r = x - hi.astype(F32)
    mid = r.astype(BF16)
    lo = (r - mid.astype(F32)).astype(BF16)
    return _dot(lo, w) + _dot(mid, w) + _dot(hi, w)


def _resident(shape):
    return pl.BlockSpec(shape, lambda *a: (0,) * len(shape), pipeline_mode=pl.Buffered(1))


def _tile(n, pref):
    return pref if n % pref == 0 else n


def _sigmoid(x):
    return 1.0 / (1.0 + jnp.exp(-x))


def _layer_norm(y, g, b):
    mu = jnp.mean(y, axis=-1, keepdims=True)
    d = y - mu
    var = jnp.mean(d * d, axis=-1, keepdims=True)
    return d * lax.rsqrt(var + LN_EPS) * g + b


def _osm(s, v, m_ref, l_ref, acc_ref):
    m_prev = m_ref[:, :1]
    m_new = jnp.maximum(m_prev, jnp.max(s, axis=1, keepdims=True))
    alpha = jnp.exp(m_prev - m_new)
    p = jnp.exp(s - m_new)
    l_ref[...] = alpha * l_ref[...] + jnp.sum(p, axis=1, keepdims=True)
    acc_ref[...] = alpha * acc_ref[...] + _dot(p.astype(BF16), v)
    m_ref[...] = jnp.broadcast_to(m_new, m_ref.shape)


def _osm_init(m_ref, l_ref, acc_ref):
    m_ref[...] = jnp.full(m_ref.shape, -jnp.inf, F32)
    l_ref[...] = jnp.zeros(l_ref.shape, F32)
    acc_ref[...] = jnp.zeros(acc_ref.shape, F32)


def _proj_kernel(x_ref, w_ref, glu, fq, fk, fv, fkb, fvb, nq, cmp_, sel, win, selb, winb, small):
    xb = x_ref[...].astype(BF16)

    def seg(a, b):
        return _dot(xb, w_ref[:, a:b])

    glu[...] = seg(C_GLU, C_FQ)
    fq[...] = seg(C_FQ, C_FK).astype(BF16)
    r = seg(C_FK, C_FV)
    fk[...] = r
    fkb[...] = r.astype(BF16)
    r = seg(C_FV, C_NQ)
    fv[...] = r
    fvb[...] = r.astype(BF16)
    nq[...] = seg(C_NQ, C_CMP).astype(BF16)
    cmp_[...] = seg(C_CMP, C_SEL)
    r = seg(C_SEL, C_WIN)
    sel[...] = r
    selb[...] = r.astype(BF16)
    r = seg(C_WIN, C_SM)
    win[...] = r
    winb[...] = r.astype(BF16)
    small[...] = seg(C_SM, C_END)


def _proj(x2, wp):
    m, d = x2.shape
    tm = _tile(m, 256)
    widths = [(C_FQ - C_GLU, F32), (C_FK - C_FQ, BF16), (FOX_W, F32), (FOX_W, F32), (FOX_W, BF16),
              (FOX_W, BF16), (C_CMP - C_NQ, BF16), (256, F32), (256, F32), (256, F32), (256, BF16),
              (256, BF16), (LANES, F32)]
    return pl.pallas_call(
        _proj_kernel,
        grid=(m // tm,),
        in_specs=[pl.BlockSpec((tm, d), lambda i: (i, 0)), _resident(wp.shape)],
        out_specs=[pl.BlockSpec((tm, w), lambda i: (i, 0)) for w, _ in widths],
        out_shape=[jax.ShapeDtypeStruct((m, w), dt) for w, dt in widths],
        compiler_params=_cparams(("parallel",)),
        name="in_proj",
    )(x2, wp)


def _gate_kernel(sm_ref, b_ref, lf_ref, crow_ref, carry_ref, *, tt, grp):
    j = pl.program_id(1)

    @pl.when((j == 0) | (grp < tt * pl.num_programs(1)))
    def _():
        carry_ref[...] = jnp.zeros(carry_ref.shape, F32)

    z = sm_ref[0] + b_ref[...]
    lf = jnp.minimum(z, 0.0) - jnp.log(1.0 + jnp.exp(-jnp.abs(z)))
    lane = lax.broadcasted_iota(jnp.int32, lf.shape, 1)
    lf = jnp.where(lane < FOX_HEADS, lf, 0.0)
    lf_ref[0] = lf
    r = lax.broadcasted_iota(jnp.int32, (tt, tt), 0)
    c = lax.broadcasted_iota(jnp.int32, (tt, tt), 1)
    tri = (c <= r) & ((r // grp) == (c // grp))
    lmat = jnp.where(tri, 1.0, 0.0).astype(BF16)
    hi = lf.astype(BF16)
    rem = lf - hi.astype(F32)
    mid = rem.astype(BF16)
    lo = (rem - mid.astype(F32)).astype(BF16)
    cs = _dot(lmat, lo) + _dot(lmat, mid) + _dot(lmat, hi) + carry_ref[0:1, :]
    carry_ref[...] = jnp.broadcast_to(cs[tt - 1:tt, :], carry_ref.shape)
    crow_ref[0] = cs.T[0:8, :]


def _gate_cumsum(small3, b_pad, grp):
    b, t, _ = small3.shape
    tt = _tile(t, 512)
    return pl.pallas_call(
        functools.partial(_gate_kernel, tt=tt, grp=grp),
        grid=(b, t // tt),
        in_specs=[pl.BlockSpec((1, tt, LANES), lambda i, j: (i, j, 0)),
                  pl.BlockSpec((1, LANES), lambda i, j: (0, 0))],
        out_specs=[pl.BlockSpec((1, tt, LANES), lambda i, j: (i, j, 0)),
                   pl.BlockSpec((1, 8, tt), lambda i, j: (i, 0, j))],
        out_shape=[jax.ShapeDtypeStruct((b, t, LANES), F32), jax.ShapeDtypeStruct((b, 8, t), F32)],
        scratch_shapes=[pltpu.VMEM((8, LANES), F32)],
        compiler_params=_cparams(("parallel", "arbitrary")),
        name="fox_gate_cumsum",
    )(small3, b_pad)


def _conv_kernel(glu_ref, st_ref, dw_ref, g_ref, b_ref, pw_ref, o_ref, so_ref, ext_ref, *, tm):
    j = pl.program_id(1)

    @pl.when(j == 0)
    def _():
        ext_ref[0:CONV_CARRY, :] = st_ref[0]

    glu = glu_ref[0]
    u = glu[:, :CONV_CH] * _sigmoid(glu[:, CONV_CH:])
    ext_ref[CONV_CARRY:CONV_CARRY + tm, :] = u
    off = CONV_CARRY - (CONV_WIDTH - 1)
    acc = jnp.zeros((tm, CONV_CH), F32)
    for k in range(CONV_WIDTH):
        acc = acc + dw_ref[k:k + 1, :] * ext_ref[pl.ds(off + k, tm), :]
    y = _layer_norm(acc, g_ref[...], b_ref[...])
    y = y * _sigmoid(y)
    o_ref[0] = _dot(y.astype(BF16), pw_ref[...])
    last = ext_ref[tm:tm + CONV_CARRY, :]
    so_ref[0] = last
    ext_ref[0:CONV_CARRY, :] = last


def _conv_module(glu3, state, dw, g, b, pw):
    bsz, t, _ = glu3.shape
    tm = _tile(t, 512)
    return pl.pallas_call(
        functools.partial(_conv_kernel, tm=tm),
        grid=(bsz, t // tm),
        in_specs=[pl.BlockSpec((1, tm, 2 * CONV_CH), lambda i, j: (i, j, 0)),
                  pl.BlockSpec((1, CONV_CARRY, CONV_CH), lambda i, j: (i, 0, 0)),
                  _resident(dw.shape), _resident(g.shape), _resident(b.shape), _resident(pw.shape)],
        out_specs=[pl.BlockSpec((1, tm, CONV_CH), lambda i, j: (i, j, 0)),
                   pl.BlockSpec((1, CONV_CARRY, CONV_CH), lambda i, j: (i, 0, 0))],
        out_shape=[jax.ShapeDtypeStruct((bsz, t, CONV_CH), F32),
                   jax.ShapeDtypeStruct((bsz, CONV_CARRY, CONV_CH), F32)],
        scratch_shapes=[pltpu.VMEM((tm + CONV_CARRY, CONV_CH), F32)],
        compiler_params=_cparams(("parallel", "arbitrary")),
        name="conv_module",
    )(glu3, state, dw, g, b, pw)


def _fox_kernel(qi_tab, ki_tab, q_ref, k_ref, v_ref, c_ref, o_ref, m_ref, l_ref, acc_ref, *, tq):
    step = pl.program_id(2)
    qi = qi_tab[step]
    ki = ki_tab[step]

    @pl.when(ki == 0)
    def _():
        _osm_init(m_ref, l_ref, acc_ref)

    q = q_ref[0]
    qs = jnp.concatenate([q[:, :LANES], q[:, LANES:]], axis=0)
    s = _nt(qs, k_ref[0])
    c = c_ref[0, 0]
    row = lax.broadcasted_iota(jnp.int32, s.shape, 0)
    s = s - jnp.where(row < tq, c[0:1, :], c[1:2, :])

    @pl.when(ki < qi)
    def _():
        _osm(s, v_ref[0], m_ref, l_ref, acc_ref)

    @pl.when(ki == qi)
    def _():
        col = lax.broadcasted_iota(jnp.int32, s.shape, 1)
        rq = jnp.where(row < tq, row, row - tq)
        _osm(jnp.where(col <= rq, s, -jnp.inf), v_ref[0], m_ref, l_ref, acc_ref)
        o = acc_ref[...] / l_ref[...]
        o_ref[0, :, :LANES] = o[:tq]
        o_ref[0, :, LANES:] = o[tq:]


def _tri_tables(nqb):
    qi = [q for q in range(nqb) for _ in range(q + 1)]
    ki = [k for q in range(nqb) for k in range(q + 1)]
    return jnp.asarray(qi, jnp.int32), jnp.asarray(ki, jnp.int32)


def _fox_prompt(fq, fkb, fvb, cpair):
    b, t, _ = fq.shape
    tq = _tile(t, 512)
    nqb = t // tq
    qi_tab, ki_tab = _tri_tables(nqb)
    grid_spec = pltpu.PrefetchScalarGridSpec(
        num_scalar_prefetch=2,
        grid=(b, FOX_HEADS // 2, int(qi_tab.shape[0])),
        in_specs=[pl.BlockSpec((1, tq, 2 * LANES), lambda i, p, s, qt, kt: (i, qt[s], p)),
                  pl.BlockSpec((1, tq, LANES), lambda i, p, s, qt, kt: (i, kt[s], p)),
                  pl.BlockSpec((1, tq, LANES), lambda i, p, s, qt, kt: (i, kt[s], p)),
                  pl.BlockSpec((1, 1, 8, tq), lambda i, p, s, qt, kt: (i, p, 0, kt[s]))],
        out_specs=pl.BlockSpec((1, tq, 2 * LANES), lambda i, p, s, qt, kt: (i, qt[s], p)),
        scratch_shapes=[pltpu.VMEM((2 * tq, LANES), F32), pltpu.VMEM((2 * tq, LANES), F32),
                        pltpu.VMEM((2 * tq, LANES), F32)])
    return pl.pallas_call(
        functools.partial(_fox_kernel, tq=tq),
        grid_spec=grid_spec,
        out_shape=jax.ShapeDtypeStruct((b, t, WIDE), F32),
        compiler_params=_cparams(("parallel", "parallel", "arbitrary")),
        name="fox_prompt",
    )(qi_tab, ki_tab, fq, fkb, fvb, cpair)


def _bucket_thresholds():
    d = np.arange(4 * MAX_DISTANCE)
    max_exact = N_BUCKETS // 2
    df = np.maximum(d, 1).astype(np.float32)
    log_b = max_exact + (np.log(df / np.float32(max_exact)) / np.float32(math.log(MAX_DISTANCE / max_exact))
                         * np.float32(N_BUCKETS - max_exact)).astype(np.int32)
    bucket = np.where(d < max_exact, d, np.minimum(log_b, N_BUCKETS - 1))
    return [int(np.argmax(bucket >= bk)) for bk in range(N_BUCKETS)]


def _bias_kernel(rb_ref, o_ref, *, q0, a, b, maxd, sub_far, tqb, rows, thr):
    h = pl.program_id(0)
    blk = pl.program_id(1)
    nk = o_ref.shape[2]
    far = rb_ref[(N_BUCKETS - 1) * NSA_HEADS + h]

    def body(r, carry):
        i = lax.broadcasted_iota(jnp.int32, (rows, nk), 0) + (q0 + blk * tqb + r * rows)
        n = lax.broadcasted_iota(jnp.int32, (rows, nk), 1)
        dist = i - (a * n + b)
        val = jnp.full((rows, nk), rb_ref[h], F32)
        for bk in range(1, N_BUCKETS):
            val = jnp.where(dist >= thr[bk], rb_ref[bk * NSA_HEADS + h], val)
        if sub_far:
            val = val - far
        ok = dist >= 0
        if maxd is not None:
            ok = ok & (dist <= maxd)
        o_ref[0, pl.ds(pl.multiple_of(r * rows, rows), rows), :] = jnp.where(ok, val, NEG)
        return carry

    lax.fori_loop(0, tqb // rows, body, 0)


def _bias_table(rb_flat, nq, nk, *, q0, a, b, maxd=None, sub_far=False):
    tqb = _tile(nq, 128)
    rows = _tile(tqb, 32)
    return pl.pallas_call(
        functools.partial(_bias_kernel, q0=q0, a=a, b=b, maxd=maxd, sub_far=sub_far, tqb=tqb, rows=rows,
                          thr=_bucket_thresholds()),
        grid=(NSA_HEADS, nq // tqb),
        in_specs=[pl.BlockSpec(memory_space=pltpu.SMEM)],
        out_specs=pl.BlockSpec((1, tqb, nk), lambda h, i: (h, i, 0)),
        out_shape=jax.ShapeDtypeStruct((NSA_HEADS, nq, nk), F32),
        compiler_params=_cparams(("parallel", "arbitrary")),
        name="t5_bias_table",
    )(rb_flat)


def _compress_finish(lead, tail, tail_next, o_ref):
    nc = lead.shape[0]
    shifted = pltpu.roll(tail, nc - 1, 0)
    row = lax.broadcasted_iota(jnp.int32, lead.shape, 0)
    o_ref[0, 0] = (lead + jnp.where(row == nc - 1, tail_next, shifted)).astype(BF16)


def _compress_kernel(x_ref, nx_ref, w_ref, o_ref, *, nc):
    j = pl.program_id(2)
    lead = jnp.zeros((nc, LANES), F32)
    tail = jnp.zeros((nc, LANES), F32)
    tail_next = jnp.zeros((1, LANES), F32)
    for r in range(CMP_STRIDE):
        rows = x_ref[0, pl.ds(r, nc, stride=CMP_STRIDE), :]
        lead = lead + rows * w_ref[r:r + 1, :]
        tail = tail + rows * w_ref[CMP_STRIDE + r:CMP_STRIDE + r + 1, :]
        tail_next = tail_next + nx_ref[0, r:r + 1, :] * w_ref[CMP_STRIDE + r:CMP_STRIDE + r + 1, :]
    tail_next = jnp.where(j == pl.num_programs(2) - 1, 0.0, tail_next)
    _compress_finish(lead, tail, tail_next, o_ref)


def _compress(cmp3, w):
    b, t, _ = cmp3.shape
    tc = _tile(t, 2048)
    nc = tc // CMP_STRIDE
    nblk = t // tc
    sub = tc // CMP_STRIDE
    return pl.pallas_call(
        functools.partial(_compress_kernel, nc=nc),
        grid=(b, 2, nblk),
        in_specs=[pl.BlockSpec((1, tc, LANES), lambda i, h, j: (i, j, h)),
                  pl.BlockSpec((1, CMP_STRIDE, LANES),
                               lambda i, h, j: (i, jnp.minimum((j + 1) * sub, nblk * sub - 1), h)),
                  pl.BlockSpec((CMP_BLOCK, LANES), lambda i, h, j: (0, h))],
        out_specs=pl.BlockSpec((1, 1, nc, LANES), lambda i, h, j: (i, h, j, 0)),
        out_shape=jax.ShapeDtypeStruct((b, 2, t // CMP_STRIDE, LANES), BF16),
        compiler_params=_cparams(("parallel", "parallel", "arbitrary")),
        name="nsa_compress",
    )(cmp3, cmp3, w)


def _compress_paged_kernel(pt_ref, *refs, npg):
    pages = refs[:npg]
    nx_ref, w_ref, o_ref = refs[npg:]
    j = pl.program_id(2)
    per = PAGE // CMP_STRIDE
    leads, tails = [], []
    for pg in pages:
        lead = jnp.zeros((per, LANES), F32)
        tail = jnp.zeros((per, LANES), F32)
        for r in range(CMP_STRIDE):
            rows = pg[0, 0, pl.ds(r, per, stride=CMP_STRIDE), :]
            lead = lead + rows * w_ref[r:r + 1, :]
            tail = tail + rows * w_ref[CMP_STRIDE + r:CMP_STRIDE + r + 1, :]
        leads.append(lead)
        tails.append(tail)
    tail_next = jnp.zeros((1, LANES), F32)
    for r in range(CMP_STRIDE):
        tail_next = tail_next + nx_ref[0, 0, r:r + 1, :] * w_ref[CMP_STRIDE + r:CMP_STRIDE + r + 1, :]
    tail_next = jnp.where(j == pl.num_programs(2) - 1, 0.0, tail_next)
    _compress_finish(jnp.concatenate(leads, axis=0), jnp.concatenate(tails, axis=0), tail_next, o_ref)


def _page_map(b, i, pt, *, j, layer, npg, last):
    return (pt[b, jnp.minimum(i * npg + j, last)], layer, 0, 0)


def _page_half_map(b, h, i, pt, *, j, layer, npg, last):
    return (pt[b, jnp.minimum(i * npg + j, last)], layer, 0, h)


def _compress_paged(cache4, page_table, layer, w):
    b, n_pages = page_table.shape
    npg = 2 * PAGES_PER_STEP if n_pages % (2 * PAGES_PER_STEP) == 0 else PAGES_PER_STEP
    nsteps = n_pages // npg
    nc = npg * PAGE // CMP_STRIDE
    pm = functools.partial(_page_half_map, layer=layer, npg=npg, last=n_pages - 1)
    in_specs = [pl.BlockSpec((1, 1, PAGE, LANES), functools.partial(pm, j=j)) for j in range(npg)]
    in_specs.append(pl.BlockSpec((1, 1, CMP_STRIDE, LANES), functools.partial(pm, j=npg)))
    in_specs.append(pl.BlockSpec((CMP_BLOCK, LANES), lambda i, h, j, pt: (0, h)))
    grid_spec = pltpu.PrefetchScalarGridSpec(
        num_scalar_prefetch=1, grid=(b, 2, nsteps), in_specs=in_specs,
        out_specs=pl.BlockSpec((1, 1, nc, LANES), lambda i, h, j, pt: (i, h, j, 0)))
    return pl.pallas_call(
        functools.partial(_compress_paged_kernel, npg=npg),
        grid_spec=grid_spec,
        out_shape=jax.ShapeDtypeStruct((b, 2, n_pages * PAGE // CMP_STRIDE, LANES), BF16),
        compiler_params=_cparams(("parallel", "parallel", "arbitrary")),
        name="nsa_compress_paged",
    )(page_table, *([cache4] * (npg + 1)), w)


def _cmp_kernel(q_ref, kc_ref, vc_ref, tb_ref, a_ref, o_ref, sel_ref, *, tq, q_start, ns, topk):
    qb = pl.program_id(2)
    q = q_ref[0]
    q3 = jnp.concatenate([q[:, r * LANES:(r + 1) * LANES] for r in range(NSA_REP)], axis=0)
    ncp = kc_ref.shape[2]
    tb = tb_ref[...].reshape(NSA_REP * tq, ncp)
    s = _nt(q3, kc_ref[0, 0]) + tb
    valid = tb > NEG_TEST
    m = jnp.max(s, axis=1, keepdims=True)
    e = jnp.where(valid, jnp.exp(s - m), 0.0)
    p = e / jnp.maximum(jnp.sum(e, axis=1, keepdims=True), 1e-30)
    o = _dot(p.astype(BF16), vc_ref[0, 0])
    for r in range(NSA_REP):
        o_ref[0, :, r * LANES:(r + 1) * LANES] = o[r * tq:(r + 1) * tq]
    imp_p = p[0:tq] + p[tq:2 * tq] + p[2 * tq:3 * tq]
    imp = _dot3(imp_p, a_ref[...])
    nsp = imp.shape[1]
    jj = lax.broadcasted_iota(jnp.int32, (tq, nsp), 1)
    t = lax.broadcasted_iota(jnp.int32, (tq, nsp), 0) + (q_start + qb * tq)
    cur = t // SEL_BLOCK
    forced = (jj == 0) | (jj == cur) | (jj == cur - 1)
    score = jnp.where(forced, 1e9, jnp.where(jj <= cur, imp, -1e9))
    score = jnp.where(jj < ns, score, -jnp.inf)
    jf = jj.astype(F32)
    chosen = jnp.zeros((tq, nsp), F32)
    for _ in range(topk):
        mx = jnp.max(score, axis=1, keepdims=True)
        first = jnp.min(jnp.where(score == mx, jf, 1e9), axis=1, keepdims=True)
        hit = jf == first
        chosen = jnp.where(hit, 1.0, chosen)
        score = jnp.where(hit, -jnp.inf, score)
    sel_ref[0, 0] = jnp.where(chosen > 0.5, 0.0, SEL_NEG).astype(BF16)


def _importance_matrix(ncp, nc, nsp):
    n = np.arange(ncp)[:, None]
    j = np.arange(nsp)[None, :]
    a = (n >= SEL_PER_CMP * j - 1) & (n <= SEL_PER_CMP * j + SEL_PER_CMP - 1) & (n < nc)
    return jnp.asarray(a, BF16)


def _nsa_cmp(nq, kvc, table, *, q_start, tk_total):
    b, t, _ = nq.shape
    ncp = kvc.shape[2]
    nc = -(-tk_total // CMP_STRIDE) - 1
    ns = -(-tk_total // SEL_BLOCK)
    nsp = -(-ns // LANES) * LANES
    tq = _tile(t, 128)
    amat = _importance_matrix(ncp, nc, nsp)
    g3 = NSA_REP * LANES
    return pl.pallas_call(
        functools.partial(_cmp_kernel, tq=tq, q_start=q_start, ns=ns, topk=min(SEL_TOPK, ns)),
        grid=(b, NSA_KV_HEADS, t // tq),
        in_specs=[pl.BlockSpec((1, tq, g3), lambda i, g, j: (i, j, g)),
                  pl.BlockSpec((1, 1, ncp, LANES), lambda i, g, j: (i, 0, 0, 0)),
                  pl.BlockSpec((1, 1, ncp, LANES), lambda i, g, j: (i, 1, 0, 0)),
                  pl.BlockSpec((NSA_REP, tq, ncp), lambda i, g, j: (g, j, 0)),
                  _resident(amat.shape)],
        out_specs=[pl.BlockSpec((1, tq, g3), lambda i, g, j: (i, j, g)),
                   pl.BlockSpec((1, 1, tq, nsp), lambda i, g, j: (i, g, j, 0))],
        out_shape=[jax.ShapeDtypeStruct((b, t, WIDE), F32),
                   jax.ShapeDtypeStruct((b, NSA_KV_HEADS, t, nsp), BF16)],
        compiler_params=_cparams(("parallel", "parallel", "arbitrary")),
        name="nsa_compressed_select",
    )(nq, kvc, kvc, table, amat)


def _sel_kernel(qi_tab, ki_tab, q_ref, sb_ref, k_ref, v_ref, oh_ref, nb_ref, o_ref,
                lhs_ref, m_ref, l_ref, acc_ref, *, tq):
    step = pl.program_id(2)
    qi = qi_tab[step]
    ki = ki_tab[step]

    @pl.when(ki == 0)
    def _():
        _osm_init(m_ref, l_ref, acc_ref)
        q = q_ref[0]
        lhs_ref[:, :LANES] = jnp.concatenate([q[:, r * LANES:(r + 1) * LANES] for r in range(NSA_REP)], axis=0)
        lhs_ref[:, LANES:] = jnp.concatenate([sb_ref[0, 0]] * NSA_REP, axis=0)

    kp = jnp.concatenate([k_ref[0], oh_ref[...]], axis=1)
    s = _nt(lhs_ref[...], kp)

    @pl.when(ki < qi - 1)
    def _():
        _osm(s, v_ref[0], m_ref, l_ref, acc_ref)

    @pl.when(ki == qi - 1)
    def _():
        _osm(s + nb_ref[:, :, :tq].reshape(NSA_REP * tq, tq), v_ref[0], m_ref, l_ref, acc_ref)

    @pl.when(ki == qi)
    def _():
        tb = nb_ref[:, :, tq:].reshape(NSA_REP * tq, tq)
        _osm(jnp.where(tb > NEG_TEST, s + tb, -jnp.inf), v_ref[0], m_ref, l_ref, acc_ref)
        o = acc_ref[...] / l_ref[...]
        for r in range(NSA_REP):
            o_ref[0, :, r * LANES:(r + 1) * LANES] = o[r * tq:(r + 1) * tq]


def _nsa_sel_prompt(nq, selbias, selkv, onehot, near):
    b, t, _ = nq.shape
    tq = near.shape[1]
    nqb = t // tq
    qi_tab, ki_tab = _tri_tables(nqb)
    g3 = NSA_REP * LANES
    grid_spec = pltpu.PrefetchScalarGridSpec(
        num_scalar_prefetch=2,
        grid=(b, NSA_KV_HEADS, int(qi_tab.shape[0])),
        in_specs=[pl.BlockSpec((1, tq, g3), lambda i, g, s, qt, kt: (i, qt[s], g)),
                  pl.BlockSpec((1, 1, tq, LANES), lambda i, g, s, qt, kt: (i, g, qt[s], 0)),
                  pl.BlockSpec((1, tq, LANES), lambda i, g, s, qt, kt: (i, kt[s], 0)),
                  pl.BlockSpec((1, tq, LANES), lambda i, g, s, qt, kt: (i, kt[s], 1)),
                  pl.BlockSpec((tq, LANES), lambda i, g, s, qt, kt: (kt[s], 0)),
                  pl.BlockSpec((NSA_REP, tq, 2 * tq), lambda i, g, s, qt, kt: (g, 0, 0))],
        out_specs=pl.BlockSpec((1, tq, g3), lambda i, g, s, qt, kt: (i, qt[s], g)),
        scratch_shapes=[pltpu.VMEM((NSA_REP * tq, 2 * LANES), BF16),
                        pltpu.VMEM((NSA_REP * tq, LANES), F32), pltpu.VMEM((NSA_REP * tq, LANES), F32),
                        pltpu.VMEM((NSA_REP * tq, LANES), F32)])
    return pl.pallas_call(
        functools.partial(_sel_kernel, tq=tq),
        grid_spec=grid_spec,
        out_shape=jax.ShapeDtypeStruct((b, t, WIDE), F32),
        compiler_params=_cparams(("parallel", "parallel", "arbitrary")),
        name="nsa_selected_prompt",
    )(qi_tab, ki_tab, nq, selbias, selkv, selkv, onehot, near)


def _win_kernel(q_ref, *refs, nkb, tq, front):
    k_refs = refs[:nkb]
    v_refs = refs[nkb:2 * nkb]
    tb_ref, o_ref = refs[2 * nkb:]
    qb = pl.program_id(2)
    q = q_ref[0]
    q3 = jnp.concatenate([q[:, r * LANES:(r + 1) * LANES] for r in range(NSA_REP)], axis=0)
    k = jnp.concatenate([r[0] for r in k_refs], axis=0) if nkb > 1 else k_refs[0][0]
    v = jnp.concatenate([r[0] for r in v_refs], axis=0) if nkb > 1 else v_refs[0][0]
    nk = k.shape[0]
    tb = tb_ref[...].reshape(NSA_REP * tq, nk)
    s = _nt(q3, k) + tb
    kl = lax.broadcasted_iota(jnp.int32, s.shape, 1) + (qb * tq - front)
    s = jnp.where((tb > NEG_TEST) & (kl >= 0), s, -jnp.inf)
    m = jnp.max(s, axis=1, keepdims=True)
    e = jnp.exp(s - m)
    p = e / jnp.sum(e, axis=1, keepdims=True)
    o = _dot(p.astype(BF16), v)
    for r in range(NSA_REP):
        o_ref[0, :, r * LANES:(r + 1) * LANES] = o[r * tq:(r + 1) * tq]


def _nsa_win(nq, winkv, table, *, tq, kblk, nkb, front):
    b, t, _ = nq.shape
    g3 = NSA_REP * LANES
    in_specs = [pl.BlockSpec((1, tq, g3), lambda i, g, j: (i, j, g))]
    for half in range(2):
        for jb in range(nkb):
            in_specs.append(pl.BlockSpec((1, kblk, LANES), functools.partial(
                lambda i, g, j, jb, half: (i, j + jb, half), jb=jb, half=half)))
    in_specs.append(pl.BlockSpec((NSA_REP, tq, nkb * kblk), lambda i, g, j: (g, 0, 0)))
    return pl.pallas_call(
        functools.partial(_win_kernel, nkb=nkb, tq=tq, front=front),
        grid=(b, NSA_KV_HEADS, t // tq),
        in_specs=in_specs,
        out_specs=pl.BlockSpec((1, tq, g3), lambda i, g, j: (i, j, g)),
        out_shape=jax.ShapeDtypeStruct((b, t, WIDE), F32),
        compiler_params=_cparams(("parallel", "parallel", "arbitrary")),
        name="nsa_window",
    )(nq, *([winkv] * (2 * nkb)), table)


def _out_kernel(oa_ref, of_ref, oc_ref, os_ref, ow_ref, sm_ref, x_ref, w_ref, g_ref, b_ref, o_ref, *, alpha):
    sg = _sigmoid(sm_ref[...])
    parts = [oa_ref[...].astype(BF16), of_ref[...].astype(BF16)]
    for h in range(NSA_HEADS):
        c0 = SM_NG + 3 * h
        lo, hi = h * LANES, (h + 1) * LANES
        o = (sg[:, c0:c0 + 1] * oc_ref[:, lo:hi] + sg[:, c0 + 1:c0 + 2] * os_ref[:, lo:hi]
             + sg[:, c0 + 2:c0 + 3] * ow_ref[:, lo:hi])
        parts.append(o.astype(BF16))
    mixed = jnp.concatenate(parts, axis=1)
    y = alpha * x_ref[...] + _dot(mixed, w_ref[...])
    o_ref[...] = _layer_norm(y, g_ref[...], b_ref[...])


def _out_ln(oa, of, oc, os_, ow, small, x2, w, g, b, alpha):
    m, d = x2.shape
    tm = _tile(m, 256)
    row = lambda width: pl.BlockSpec((tm, width), lambda i: (i, 0))
    return pl.pallas_call(
        functools.partial(_out_kernel, alpha=alpha),
        grid=(m // tm,),
        in_specs=[row(CONV_CH), row(WIDE), row(WIDE), row(WIDE), row(WIDE), row(LANES), row(d),
                  _resident(w.shape), _resident(g.shape), _resident(b.shape)],
        out_specs=row(d),
        out_shape=jax.ShapeDtypeStruct((m, d), F32),
        compiler_params=_cparams(("parallel",)),
        name="out_proj_ln",
    )(oa, of, oc, os_, ow, small, x2, w, g, b)


def _ffn_kernel(x_ref, st_ref, wi_ref, wo_ref, dw_ref, g_ref, b_ref, o_ref, so_ref, carry_ref, ext_ref,
                *, tm, dff, fc, alpha):
    j = pl.program_id(1)

    @pl.when(j == 0)
    def _():
        carry_ref[...] = st_ref[0]

    x = x_ref[0]
    xb = x.astype(BF16)
    acc = jnp.zeros(x.shape, F32)
    for c in range(dff // fc):
        lo, hi = c * fc, (c + 1) * fc
        up = _dot(xb, wi_ref[:, lo:hi])
        gate = _dot(xb, wi_ref[:, dff + lo:dff + hi])
        ext_ref[0:FFN_CARRY, :] = carry_ref[:, lo:hi]
        ext_ref[FFN_CARRY:FFN_CARRY + tm, :] = up
        y = (dw_ref[2:3, lo:hi] * up + dw_ref[1:2, lo:hi] * ext_ref[pl.ds(FFN_CARRY - 1, tm), :]
             + dw_ref[0:1, lo:hi] * ext_ref[pl.ds(FFN_CARRY - 2, tm), :])
        carry_ref[:, lo:hi] = ext_ref[tm:tm + FFN_CARRY, :]
        gelu = 0.5 * y * (1.0 + jnp.tanh(math.sqrt(2.0 / math.pi) * (y + 0.044715 * (y * y * y))))
        acc = acc + _dot((gelu * gate).astype(BF16), wo_ref[lo:hi, :])
    so_ref[0] = carry_ref[...]
    o_ref[0] = _layer_norm(alpha * x + acc, g_ref[...], b_ref[...])


def _ffn(x3, state, wi, wo, dw, g, b, alpha):
    bsz, t, d = x3.shape
    dff = wo.shape[0]
    tm = _tile(t, 256)
    fc = 256
    return pl.pallas_call(
        functools.partial(_ffn_kernel, tm=tm, dff=dff, fc=fc, alpha=alpha),
        grid=(bsz, t // tm),
        in_specs=[pl.BlockSpec((1, tm, d), lambda i, j: (i, j, 0)),
                  pl.BlockSpec((1, FFN_CARRY, dff), lambda i, j: (i, 0, 0)),
                  _resident(wi.shape), _resident(wo.shape), _resident(dw.shape),
                  _resident(g.shape), _resident(b.shape)],
        out_specs=[pl.BlockSpec((1, tm, d), lambda i, j: (i, j, 0)),
                   pl.BlockSpec((1, FFN_CARRY, dff), lambda i, j: (i, 0, 0))],
        out_shape=[jax.ShapeDtypeStruct((bsz, t, d), F32), jax.ShapeDtypeStruct((bsz, FFN_CARRY, dff), F32)],
        scratch_shapes=[pltpu.VMEM((FFN_CARRY, dff), F32), pltpu.VMEM((tm + FFN_CARRY, fc), F32)],
        compiler_params=_cparams(("parallel", "arbitrary"), vmem_mb=56),
        name="conv_ffn_ln",
    )(x3, state, wi, wo, dw, g, b)


def _fox_dec_kernel(pt_ref, q_ref, u_ref, kn_ref, vn_ref, cn_ref, *refs, npg):
    kp = refs[:npg]
    vp = refs[npg:2 * npg]
    lp = refs[2 * npg:3 * npg]
    o_ref, m_ref, l_ref, acc_ref, carry_ref = refs[3 * npg:]
    i = pl.program_id(1)

    @pl.when(i == 0)
    def _():
        _osm_init(m_ref, l_ref, acc_ref)
        carry_ref[...] = jnp.zeros(carry_ref.shape, F32)

    q = q_ref[0]
    lf = jnp.concatenate([r[0, 0] for r in lp], axis=0)
    cw = _dot3(lf, u_ref[...])
    carry = carry_ref[...]
    scores = []
    for j in range(npg):
        cj = cw[8 * j:8 * j + 8] + carry
        carry = carry + cw[8 * j:8 * j + 8, LANES - 1:LANES]
        kj = kp[j][0, 0].astype(BF16)
        scores.append(_nt(q, kj) - jnp.concatenate([cj] * 8, axis=0))
    carry_ref[...] = carry
    v = jnp.concatenate([r[0, 0].astype(BF16) for r in vp], axis=0)
    _osm(jnp.concatenate(scores, axis=1), v, m_ref, l_ref, acc_ref)

    @pl.when(i == pl.num_programs(1) - 1)
    def _():
        sn = _nt(q, kn_ref[0].astype(BF16)) - jnp.concatenate([carry + cn_ref[0]] * 8, axis=0)
        col = lax.broadcasted_iota(jnp.int32, sn.shape, 1)
        qpos = lax.broadcasted_iota(jnp.int32, sn.shape, 0) // 8
        _osm(jnp.where(col <= qpos, sn, -jnp.inf), vn_ref[0].astype(BF16), m_ref, l_ref, acc_ref)
        o_ref[0] = acc_ref[...] / l_ref[:, :1]


def _fox_decode(qbd, cache_k4, cache_v4, lf_t, page_table, layer, knew, vnew, cnew):
    b, n_pages = page_table.shape
    npg = PAGES_PER_STEP
    pm = functools.partial(_page_map, layer=layer, npg=npg, last=n_pages - 1)
    umat = jnp.asarray(np.triu(np.ones((LANES, LANES), np.float32)), BF16)
    in_specs = [pl.BlockSpec((1, 64, FOX_W), lambda i, j, pt: (i, 0, 0)),
                _resident(umat.shape),
                pl.BlockSpec((1, PAGE, FOX_W), lambda i, j, pt: (i, 0, 0)),
                pl.BlockSpec((1, PAGE, FOX_W), lambda i, j, pt: (i, 0, 0)),
                pl.BlockSpec((1, 8, LANES), lambda i, j, pt: (i, 0, 0))]
    in_specs += [pl.BlockSpec((1, 1, PAGE, FOX_W), functools.partial(pm, j=j)) for j in range(npg)]
    in_specs += [pl.BlockSpec((1, 1, PAGE, FOX_W), functools.partial(pm, j=j)) for j in range(npg)]
    in_specs += [pl.BlockSpec((1, 1, 8, LANES), functools.partial(pm, j=j)) for j in range(npg)]
    grid_spec = pltpu.PrefetchScalarGridSpec(
        num_scalar_prefetch=1, grid=(b, n_pages // npg), in_specs=in_specs,
        out_specs=pl.BlockSpec((1, 64, FOX_W), lambda i, j, pt: (i, 0, 0)),
        scratch_shapes=[pltpu.VMEM((64, LANES), F32), pltpu.VMEM((64, LANES), F32),
                        pltpu.VMEM((64, FOX_W), F32), pltpu.VMEM((8, LANES), F32)])
    return pl.pallas_call(
        functools.partial(_fox_dec_kernel, npg=npg),
        grid_spec=grid_spec,
        out_shape=jax.ShapeDtypeStruct((b, 64, FOX_W), F32),
        compiler_params=_cparams(("parallel", "arbitrary")),
        name="fox_decode",
    )(page_table, qbd, umat, knew, vnew, cnew, *([cache_k4] * npg), *([cache_v4] * npg), *([lf_t] * npg))


def _sel_dec_kernel(pt_ref, q_ref, ss_ref, e_ref, tp_ref, tn_ref, sn_ref, kvn_ref, *refs, npg):
    pages = refs[:npg]
    o_ref, m_ref, l_ref, acc_ref = refs[npg:]
    i = pl.program_id(1)
    last = i == pl.num_programs(1) - 1

    @pl.when(i == 0)
    def _():
        _osm_init(m_ref, l_ref, acc_ref)

    q = q_ref[0]
    k = jnp.concatenate([r[0, 0, :, :LANES].astype(BF16) for r in pages], axis=0)
    v = jnp.concatenate([r[0, 0, :, LANES:].astype(BF16) for r in pages], axis=0)
    s = _nt(q, k) + _dot(ss_ref[0, 0], e_ref[...]) + jnp.where(last, tp_ref[...], 0.0)
    _osm(s, v, m_ref, l_ref, acc_ref)

    @pl.when(last)
    def _():
        kvn = kvn_ref[0]
        tn = tn_ref[...]
        sn = _nt(q, kvn[:, :LANES]) + tn + sn_ref[0]
        _osm(jnp.where(tn > NEG_TEST, sn, -jnp.inf), kvn[:, LANES:], m_ref, l_ref, acc_ref)
        o_ref[0] = acc_ref[...] / l_ref[...]


def _nsa_sel_decode(q48, selstep, emat, tpast, tnew, selnew, kvnew, cache4, page_table, layer):
    b, n_pages = page_table.shape
    npg = PAGES_PER_STEP
    pm = functools.partial(_page_map, layer=layer, npg=npg, last=n_pages - 1)
    rows = NSA_HEADS * 8
    in_specs = [pl.BlockSpec((1, rows, LANES), lambda i, j, pt: (i, 0, 0)),
                pl.BlockSpec((1, 1, rows, 2 * npg), lambda i, j, pt: (i, j, 0, 0)),
                _resident(emat.shape), _resident(tpast.shape), _resident(tnew.shape),
                pl.BlockSpec((1, rows, LANES), lambda i, j, pt: (i, 0, 0)),
                pl.BlockSpec((1, PAGE, 2 * LANES), lambda i, j, pt: (i, 0, 0))]
    in_specs += [pl.BlockSpec((1, 1, PAGE, 2 * LANES), functools.partial(pm, j=j)) for j in range(npg)]
    grid_spec = pltpu.PrefetchScalarGridSpec(
        num_scalar_prefetch=1, grid=(b, n_pages // npg), in_specs=in_specs,
        out_specs=pl.BlockSpec((1, rows, LANES), lambda i, j, pt: (i, 0, 0)),
        scratch_shapes=[pltpu.VMEM((rows, LANES), F32), pltpu.VMEM((rows, LANES), F32),
                        pltpu.VMEM((rows, LANES), F32)])
    return pl.pallas_call(
        functools.partial(_sel_dec_kernel, npg=npg),
        grid_spec=grid_spec,
        out_shape=jax.ShapeDtypeStruct((b, rows, LANES), F32),
        compiler_params=_cparams(("parallel", "arbitrary")),
        name="nsa_selected_decode",
    )(page_table, q48, selstep, emat, tpast, tnew, selnew, kvnew, *([cache4] * npg))


def _in_col_map():
    src = -np.ones(C_END, np.int64)
    scale = np.ones(C_END, np.float32)
    o_fq, o_fk, o_fv, o_ff = 2 * CONV_CH, 2 * CONV_CH + FOX_W, 2 * CONV_CH + 2 * FOX_W, 2 * CONV_CH + 3 * FOX_W
    o_nq = o_ff + FOX_HEADS
    o_nkv = o_nq + NSA_W
    o_ng = o_nkv + 6 * NSA_KV_W
    src[C_GLU:C_FQ] = np.arange(2 * CONV_CH)
    d = np.arange(HEAD_DIM)
    for h in range(FOX_HEADS):
        dst = C_FQ + LANES * h + HEAD_DIM * (h % 2)
        src[dst:dst + HEAD_DIM] = o_fq + HEAD_DIM * h + d
        scale[dst:dst + HEAD_DIM] = HEAD_DIM ** -0.5
    src[C_FK:C_FV] = o_fk + np.arange(FOX_W)
    src[C_FV:C_NQ] = o_fv + np.arange(FOX_W)
    for h in range(NSA_HEADS):
        dst = C_NQ + LANES * h + HEAD_DIM * (h // NSA_REP)
        src[dst:dst + HEAD_DIM] = o_nq + HEAD_DIM * h + d
        scale[dst:dst + HEAD_DIM] = HEAD_DIM ** -0.5
    src[C_CMP:C_SM] = o_nkv + np.arange(6 * NSA_KV_W)
    src[C_SM + SM_FF:C_SM + SM_FF + FOX_HEADS] = o_ff + np.arange(FOX_HEADS)
    src[C_SM + SM_NG:C_SM + SM_NG + 3 * NSA_HEADS] = o_ng + np.arange(3 * NSA_HEADS)
    return src, scale


def _out_row_map():
    n = CONV_CH + 2 * WIDE
    src = -np.ones(n, np.int64)
    src[:CONV_CH] = np.arange(CONV_CH)
    d = np.arange(HEAD_DIM)
    for h in range(FOX_HEADS):
        dst = CONV_CH + LANES * h + HEAD_DIM * (h % 2)
        src[dst:dst + HEAD_DIM] = CONV_CH + HEAD_DIM * h + d
    for h in range(NSA_HEADS):
        dst = CONV_CH + WIDE + LANES * h + HEAD_DIM * (h // NSA_REP)
        src[dst:dst + HEAD_DIM] = CONV_CH + FOX_W + HEAD_DIM * h + d
    return src


def _prep_layer(l, p):
    src, scale = _in_col_map()
    w_in = p['w_in'][l]
    wp = jnp.where(jnp.asarray(src >= 0)[None, :], jnp.take(w_in, jnp.asarray(np.maximum(src, 0)), axis=1), 0.0)
    wp = (wp * jnp.asarray(scale)[None, :]).astype(BF16)
    rsrc = _out_row_map()
    w_out = p['w_out'][l]
    wo = jnp.where(jnp.asarray(rsrc >= 0)[:, None], jnp.take(w_out, jnp.asarray(np.maximum(rsrc, 0)), axis=0), 0.0)
    dw = p['conv_dw'][l]
    fdw = p['ffn_dw'][l]
    return dict(
        wp=wp, wo=wo.astype(BF16),
        b_f=jnp.zeros((1, LANES), F32).at[0, SM_FF:SM_FF + FOX_HEADS].set(p['b_fgate'][l]),
        conv_dw=jnp.pad(dw, ((0, CONV_CARRY - dw.shape[0]), (0, 0))),
        conv_g=p['conv_ln_g'][l][None, :], conv_b=p['conv_ln_b'][l][None, :],
        conv_pw=p['conv_pw'][l].astype(BF16),
        cmp_w=jnp.transpose(p['nsa_cmp_w'][l], (1, 0, 2, 3)).reshape(CMP_BLOCK, 2 * NSA_KV_W),
        ln1_g=p['ln1_g'][l][None, :], ln1_b=p['ln1_b'][l][None, :],
        ffn_wi=p['ffn_w_in'][l].astype(BF16), ffn_wo=p['ffn_w_out'][l].astype(BF16),
        ffn_dw=jnp.pad(fdw, ((0, 8 - fdw.shape[0]), (0, 0))),
        ln2_g=p['ln2_g'][l][None, :], ln2_b=p['ln2_b'][l][None, :])


def _wide_from_heads(o, halves):
    z = jnp.zeros_like(o)
    sel = jnp.asarray(halves, jnp.int32)[:, None] == 0
    return jnp.concatenate([jnp.where(sel, o, z), jnp.where(sel, z, o)], axis=-1).reshape(o.shape[:-2] + (WIDE,))


def _layer_prompt(x, l, lw, tabs, alpha):
    b, t, d = x.shape
    m = b * t
    (glu, fq, fk, fv, fkb, fvb, nq, cmp_, sel, win, selb16, winb16, small) = _proj(x.reshape(m, d), lw['wp'])
    r3 = lambda a: a.reshape(b, t, a.shape[-1])
    out_a, conv_state = _conv_module(r3(glu), jnp.zeros((b, CONV_CARRY, CONV_CH), F32), lw['conv_dw'],
                                     lw['conv_g'], lw['conv_b'], lw['conv_pw'])
    lf, crow = _gate_cumsum(r3(small), lw['b_f'], t)
    cpair = jnp.pad(crow[:, :FOX_HEADS].reshape(b, FOX_HEADS // 2, 2, t), ((0, 0), (0, 0), (0, 6), (0, 0)))
    o_fox = _fox_prompt(r3(fq), r3(fkb), r3(fvb), cpair)
    kvc = _compress(r3(cmp_), lw['cmp_w'])
    nq3 = r3(nq)
    o_cmp, selbias = _nsa_cmp(nq3, kvc, tabs['cmp'], q_start=0, tk_total=t)
    o_sel = _nsa_sel_prompt(nq3, selbias, r3(selb16), tabs['onehot'], tabs['near'])
    tqw = tabs['win'].shape[1]
    winpad = jnp.pad(r3(winb16), ((0, 0), (WINDOW, 0), (0, 0)))
    o_win = _nsa_win(nq3, winpad, tabs['win'], tq=tqw, kblk=tqw, nkb=WINDOW // tqw + 1, front=WINDOW)
    x1 = _out_ln(out_a.reshape(m, CONV_CH), o_fox.reshape(m, WIDE), o_cmp.reshape(m, WIDE),
                 o_sel.reshape(m, WIDE), o_win.reshape(m, WIDE), small, x.reshape(m, d),
                 lw['wo'], lw['ln1_g'], lw['ln1_b'], alpha)
    dff = lw['ffn_wo'].shape[0]
    y, ffn_state = _ffn(x1.reshape(b, t, d), jnp.zeros((b, FFN_CARRY, dff), F32), lw['ffn_wi'], lw['ffn_wo'],
                        lw['ffn_dw'], lw['ln2_g'], lw['ln2_b'], alpha)
    buf = min(WINDOW, t)
    new = dict(fox_k=fk.reshape(b, t, FOX_HEADS, HEAD_DIM), fox_v=fv.reshape(b, t, FOX_HEADS, HEAD_DIM),
               fox_logf=lf[:, :, :FOX_HEADS],
               nsa_cmp=cmp_.reshape(b, t, 2, NSA_KV_HEADS, HEAD_DIM),
               nsa_sel=sel.reshape(b, t, 2, NSA_KV_HEADS, HEAD_DIM),
               nsa_win=r3(win)[:, t - buf:].reshape(b, buf, 2, NSA_KV_HEADS, HEAD_DIM),
               conv=conv_state[:, CONV_CARRY - (CONV_WIDTH - 1):],
               ffn_conv=ffn_state[:, FFN_CARRY - (FFN_CONV_WIDTH - 1):])
    return y, new


def _layer_sample(x, l, lw, tabs, alpha, cache):
    b, t, d = x.shape
    m = b * t
    pt = cache['page_table']
    n_pages = pt.shape[1]
    past = n_pages * PAGE
    (glu, fq, fk, fv, fkb, fvb, nq, cmp_, sel, win, selb16, winb16, small) = _proj(x.reshape(m, d), lw['wp'])
    r3 = lambda a: a.reshape(b, t, a.shape[-1])
    st = jnp.pad(cache['conv'][l], ((0, 0), (CONV_CARRY - (CONV_WIDTH - 1), 0), (0, 0)))
    out_a, conv_state = _conv_module(r3(glu), st, lw['conv_dw'], lw['conv_g'], lw['conv_b'], lw['conv_pw'])
    lf, crow = _gate_cumsum(small.reshape(1, m, LANES), lw['b_f'], t)
    lf = lf.reshape(b, t, LANES)
    cnew = jnp.pad(jnp.transpose(crow[0].reshape(8, b, t), (1, 0, 2)), ((0, 0), (0, 0), (0, LANES - t)))
    q6 = r3(fq).reshape(b, t, FOX_HEADS, 2, HEAD_DIM)
    qc = jnp.where((jnp.arange(FOX_HEADS) % 2 == 0)[:, None], q6[:, :, :, 0], q6[:, :, :, 1])
    eye = jnp.eye(8, FOX_HEADS, dtype=qc.dtype)
    qbd = (jnp.pad(qc, ((0, 0), (0, 0), (0, 2), (0, 0)))[:, :, :, None, :] * eye[None, None, :, :, None])
    qbd = qbd.reshape(b, t * 8, FOX_W)
    padrows = lambda a: jnp.pad(a, ((0, 0), (0, PAGE - t), (0, 0)))
    o64 = _fox_decode(qbd, cache['fox_k4'], cache['fox_v4'], cache['lf_t'], pt, l,
                      padrows(r3(fk)), padrows(r3(fv)), cnew)
    o5 = o64.reshape(b, t, 8, FOX_HEADS, HEAD_DIM)[:, :, :FOX_HEADS]
    o_f = jnp.moveaxis(jnp.diagonal(o5, axis1=2, axis2=3), -1, 2)
    o_fox = _wide_from_heads(o_f, [h % 2 for h in range(FOX_HEADS)])
    kvc = _compress_paged(cache['cmp4'], pt, l, lw['cmp_w'])
    nq3 = r3(nq)
    o_cmp, selbias = _nsa_cmp(nq3, kvc, tabs['cmp'], q_start=past, tk_total=past + t)
    npg = PAGES_PER_STEP
    nsteps = n_pages // npg
    sb = selbias[:, :, :, :2 * n_pages].reshape(b, NSA_KV_HEADS, 1, t, nsteps, 2 * npg)
    sb = jnp.broadcast_to(sb, (b, NSA_KV_HEADS, NSA_REP, t, nsteps, 2 * npg))
    selstep = jnp.transpose(sb, (0, 4, 1, 2, 3, 5)).reshape(b, nsteps, NSA_HEADS * t, 2 * npg)
    sn = selbias[:, :, :, 2 * n_pages].astype(F32)
    selnew = jnp.broadcast_to(sn[:, :, None, :, None], (b, NSA_KV_HEADS, NSA_REP, t, LANES))
    selnew = selnew.reshape(b, NSA_HEADS * t, LANES)
    q48 = jnp.transpose(nq3.reshape(b, t, NSA_HEADS, LANES), (0, 2, 1, 3)).reshape(b, NSA_HEADS * t, LANES)
    o48 = _nsa_sel_decode(q48, selstep, tabs['emat'], tabs['sel_past'], tabs['sel_new'], selnew,
                          padrows(r3(selb16)), cache['sel4'], pt, l)
    o_sel = jnp.transpose(o48.reshape(b, NSA_HEADS, t, LANES), (0, 2, 1, 3)).reshape(b, t, WIDE)
    win_all = jnp.concatenate([cache['nsa_win'][l].reshape(b, -1, 2 * NSA_KV_W), r3(win)], axis=1)
    wrows = tabs['win'].shape[2]
    winpad = jnp.pad(win_all.astype(BF16), ((0, 0), (0, wrows - win_all.shape[1]), (0, 0)))
    o_win = _nsa_win(nq3, winpad, tabs['win'], tq=t, kblk=wrows, nkb=1, front=0)
    x1 = _out_ln(out_a.reshape(m, CONV_CH), o_fox.reshape(m, WIDE), o_cmp.reshape(m, WIDE),
                 o_sel.reshape(m, WIDE), o_win.reshape(m, WIDE), small, x.reshape(m, d),
                 lw['wo'], lw['ln1_g'], lw['ln1_b'], alpha)
    fst = jnp.pad(cache['ffn_conv'][l], ((0, 0), (FFN_CARRY - (FFN_CONV_WIDTH - 1), 0), (0, 0)))
    y, ffn_state = _ffn(x1.reshape(b, t, d), fst, lw['ffn_wi'], lw['ffn_wo'], lw['ffn_dw'],
                        lw['ln2_g'], lw['ln2_b'], alpha)
    wlen = cache['nsa_win'].shape[2]
    new = dict(fox_k=fk.reshape(b, t, FOX_HEADS, HEAD_DIM), fox_v=fv.reshape(b, t, FOX_HEADS, HEAD_DIM),
               fox_logf=lf[:, :, :FOX_HEADS],
               nsa_cmp=cmp_.reshape(b, t, 2, NSA_KV_HEADS, HEAD_DIM),
               nsa_sel=sel.reshape(b, t, 2, NSA_KV_HEADS, HEAD_DIM),
               nsa_win=win_all[:, win_all.shape[1] - wlen:].reshape(b, wlen, 2, NSA_KV_HEADS, HEAD_DIM),
               conv=conv_state[:, CONV_CARRY - (CONV_WIDTH - 1):],
               ffn_conv=ffn_state[:, FFN_CARRY - (FFN_CONV_WIDTH - 1):])
    return y, new


ROW_KEYS = ('fox_k', 'fox_v', 'fox_logf', 'nsa_cmp', 'nsa_sel')
STATE_KEYS = ('nsa_win', 'conv', 'ffn_conv')


def _collect(news):
    out = {k: jnp.stack([n[k] for n in news], axis=1) for k in ROW_KEYS}
    out.update({k: jnp.stack([n[k] for n in news], axis=0) for k in STATE_KEYS})
    return out


def kernel(x_prompt, x_sample, cache_fox_k, cache_fox_v, cache_fox_logf, cache_nsa_cmp, cache_nsa_sel, state_nsa_win, state_conv, state_ffn_conv, page_table, w_in, b_fgate, conv_dw, conv_ln_g, conv_ln_b, conv_pw, nsa_cmp_w, rel_bias, w_out, ln1_g, ln1_b, ffn_w_in, ffn_dw, ffn_w_out, ln2_g, ln2_b):
    p = dict(w_in=w_in, b_fgate=b_fgate, conv_dw=conv_dw, conv_ln_g=conv_ln_g, conv_ln_b=conv_ln_b,
             conv_pw=conv_pw, nsa_cmp_w=nsa_cmp_w, rel_bias=rel_bias, w_out=w_out, ln1_g=ln1_g, ln1_b=ln1_b,
             ffn_w_in=ffn_w_in, ffn_dw=ffn_dw, ffn_w_out=ffn_w_out, ln2_g=ln2_g, ln2_b=ln2_b)
    depth = w_in.shape[0]
    alpha = (2 * depth) ** 0.25
    lws = [_prep_layer(l, p) for l in range(depth)]
    rb_flat = rel_bias.reshape(-1)

    b, t, _ = x_prompt.shape
    tq_sel = _tile(t, 256)
    tq_win = _tile(t, 256)
    tabs = dict(
        cmp=_bias_table(rb_flat, t, t // CMP_STRIDE, q0=0, a=CMP_STRIDE, b=CMP_BLOCK - 1),
        near=_bias_table(rb_flat, tq_sel, 2 * tq_sel, q0=0, a=1, b=-tq_sel, sub_far=True),
        win=_bias_table(rb_flat, tq_win, WINDOW + tq_win, q0=0, a=1, b=-WINDOW, maxd=WINDOW),
        onehot=jnp.asarray((np.arange(t)[:, None] // SEL_BLOCK) == np.arange(LANES)[None, :], BF16))
    x = x_prompt
    news = []
    for l in range(depth):
        x, new = _layer_prompt(x, l, lws[l], tabs, alpha)
        news.append(new)
    y_prompt, sp = x, _collect(news)

    db, dt, _ = x_sample.shape
    n_pool = cache_fox_k.shape[0]
    n_pages = page_table.shape[1]
    past = n_pages * PAGE
    npg = PAGES_PER_STEP
    wrows = -(-(state_nsa_win.shape[2] + dt) // LANES) * LANES
    near_s = _bias_table(rb_flat, dt, 2 * LANES, q0=0, a=1, b=-LANES, sub_far=True).reshape(NSA_HEADS * dt, 2 * LANES)
    tabs_s = dict(
        cmp=_bias_table(rb_flat, dt, past // CMP_STRIDE, q0=past, a=CMP_STRIDE, b=CMP_BLOCK - 1),
        win=_bias_table(rb_flat, dt, wrows, q0=state_nsa_win.shape[2], a=1, b=0, maxd=WINDOW),
        sel_past=jnp.pad(near_s[:, :LANES], ((0, 0), ((npg - 1) * PAGE, 0))),
        sel_new=near_s[:, LANES:],
        emat=jnp.asarray((np.arange(npg * PAGE)[None, :] // SEL_BLOCK) == np.arange(2 * npg)[:, None], BF16))
    lf_t = jnp.pad(jnp.swapaxes(cache_fox_logf, 2, 3), ((0, 0), (0, 0), (0, 8 - FOX_HEADS), (0, 0)))
    cache = dict(
        page_table=page_table,
        fox_k4=cache_fox_k.reshape(n_pool, depth, PAGE, FOX_W),
        fox_v4=cache_fox_v.reshape(n_pool, depth, PAGE, FOX_W),
        lf_t=lf_t,
        cmp4=cache_nsa_cmp.reshape(n_pool, depth, PAGE, 2 * NSA_KV_W),
        sel4=cache_nsa_sel.reshape(n_pool, depth, PAGE, 2 * NSA_KV_W),
        nsa_win=state_nsa_win, conv=state_conv, ffn_conv=state_ffn_conv)
    x = x_sample
    news = []
    for l in range(depth):
        x, new = _layer_sample(x, l, lws[l], tabs_s, alpha, cache)
        news.append(new)
    y_sample, ss = x, _collect(news)

    return (y_prompt, y_sample,
            sp['fox_k'], sp['fox_v'], sp['fox_logf'], sp['nsa_cmp'], sp['nsa_sel'], sp['nsa_win'], sp['conv'], sp['ffn_conv'],
            ss['fox_k'], ss['fox_v'], ss['fox_logf'], ss['nsa_cmp'], ss['nsa_sel'], ss['nsa_win'], ss['conv'], ss['ffn_conv'])
```

```python
import functools
import math

import numpy as np
import jax
import jax.numpy as jnp
from jax import lax
from jax.experimental import pallas as pl
from jax.experimental.pallas import tpu as pltpu

F32 = jnp.float32
BF16 = jnp.bfloat16

HEAD_DIM = 64
LANES = 128
CONV_CH = 256
CONV_WIDTH = 31
CONV_CARRY = 32
FOX_HEADS = 6
NSA_HEADS = 6
NSA_KV_HEADS = 2
NSA_REP = NSA_HEADS // NSA_KV_HEADS
CMP_BLOCK = 32
CMP_STRIDE = 16
SEL_BLOCK = 64
SEL_TOPK = 16
SEL_PER_CMP = SEL_BLOCK // CMP_STRIDE
WINDOW = 512
N_BUCKETS = 32
MAX_DISTANCE = 128
FFN_CONV_WIDTH = 3
FFN_CARRY = 8
LN_EPS = 1e-5
PAGE = 128
PAGES_PER_STEP = 8

FOX_W = FOX_HEADS * HEAD_DIM
NSA_W = NSA_HEADS * HEAD_DIM
NSA_KV_W = NSA_KV_HEADS * HEAD_DIM
WIDE = 6 * LANES

NEG = -1e30
NEG_TEST = -1e29
SEL_NEG = -32768.0

C_GLU, C_FQ, C_FK, C_FV, C_NQ, C_CMP, C_SEL, C_WIN, C_SM, C_END = (
    0, 512, 1280, 1664, 2048, 2816, 3072, 3328, 3584, 3712)
SM_FF = 0
SM_NG = 6


def _cparams(sem, vmem_mb=48):
    return pltpu.CompilerParams(dimension_semantics=sem, vmem_limit_bytes=vmem_mb * 1024 * 1024)


def _dot(a, b):
    return jnp.dot(a, b, preferred_element_type=F32)


def _nt(a, b):
    return lax.dot_general(a, b, (((1,), (1,)), ((), ())), preferred_element_type=F32)


def _dot3(x, w):
    hi = x.astype(BF16)
    r = x - hi.astype(F32)
    mid = r.astype(BF16)
    lo = (r - mid.astype(F32)).astype(BF16)
    return _dot(lo, w) + _dot(mid, w) + _dot(hi, w)


def _resident(shape):
    return pl.BlockSpec(shape, lambda *a: (0,) * len(shape), pipeline_mode=pl.Buffered(1))


def _tile(n, pref):
    return pref if n % pref == 0 else n


def _sigmoid(x):
    return 1.0 / (1.0 + jnp.exp(-x))


def _layer_norm(y, g, b):
    mu = jnp.mean(y, axis=-1, keepdims=True)
    d = y - mu
    var = jnp.mean(d * d, axis=-1, keepdims=True)
    return d * lax.rsqrt(var + LN_EPS) * g + b


def _osm(s, v, m_ref, l_ref, acc_ref, v_transposed=False):
    m_prev = m_ref[:, :1]
    m_new = jnp.maximum(m_prev, jnp.max(s, axis=1, keepdims=True))
    alpha = jnp.exp(m_prev - m_new)
    p = jnp.exp(s - m_new)
    l_ref[...] = alpha * l_ref[...] + jnp.sum(p, axis=1, keepdims=True)
    pb = p.astype(BF16)
    acc_ref[...] = alpha * acc_ref[...] + (_nt(pb, v) if v_transposed else _dot(pb, v))
    m_ref[...] = jnp.broadcast_to(m_new, m_ref.shape)


def _osm_init(m_ref, l_ref, acc_ref):
    m_ref[...] = jnp.full(m_ref.shape, -jnp.inf, F32)
    l_ref[...] = jnp.zeros(l_ref.shape, F32)
    acc_ref[...] = jnp.zeros(acc_ref.shape, F32)


def _proj_kernel(x_ref, w_ref, glu, fq, fk, fv, fkb, fvb, nq, cmp_, sel, win, selb, winb, small):
    xb = x_ref[...].astype(BF16)

    def seg(a, b):
        return _dot(xb, w_ref[:, a:b])

    glu[...] = seg(C_GLU, C_FQ)
    fq[...] = seg(C_FQ, C_FK).astype(BF16)
    r = seg(C_FK, C_FV)
    fk[...] = r
    fkb[...] = r.astype(BF16)
    r = seg(C_FV, C_NQ)
    fv[...] = r
    fvb[...] = r.astype(BF16)
    nq[...] = seg(C_NQ, C_CMP).astype(BF16)
    cmp_[...] = seg(C_CMP, C_SEL)
    r = seg(C_SEL, C_WIN)
    sel[...] = r
    selb[...] = r.astype(BF16)
    r = seg(C_WIN, C_SM)
    win[...] = r
    winb[...] = r.astype(BF16)
    small[...] = seg(C_SM, C_END)


def _proj(x2, wp):
    m, d = x2.shape
    tm = _tile(m, 256)
    widths = [(C_FQ - C_GLU, F32), (C_FK - C_FQ, BF16), (FOX_W, F32), (FOX_W, F32), (FOX_W, BF16),
              (FOX_W, BF16), (C_CMP - C_NQ, BF16), (256, F32), (256, F32), (256, F32), (256, BF16),
              (256, BF16), (LANES, F32)]
    return pl.pallas_call(
        _proj_kernel,
        grid=(m // tm,),
        in_specs=[pl.BlockSpec((tm, d), lambda i: (i, 0)), _resident(wp.shape)],
        out_specs=[pl.BlockSpec((tm, w), lambda i: (i, 0)) for w, _ in widths],
        out_shape=[jax.ShapeDtypeStruct((m, w), dt) for w, dt in widths],
        compiler_params=_cparams(("parallel",)),
        name="in_proj",
    )(x2, wp)


def _gate_kernel(sm_ref, b_ref, lf_ref, crow_ref, carry_ref, *, tt, grp):
    j = pl.program_id(1)

    @pl.when((j == 0) | (grp < tt * pl.num_programs(1)))
    def _():
        carry_ref[...] = jnp.zeros(carry_ref.shape, F32)

    z = sm_ref[0] + b_ref[...]
    lf = jnp.minimum(z, 0.0) - jnp.log(1.0 + jnp.exp(-jnp.abs(z)))
    lane = lax.broadcasted_iota(jnp.int32, lf.shape, 1)
    lf = jnp.where(lane < FOX_HEADS, lf, 0.0)
    lf_ref[0] = lf
    r = lax.broadcasted_iota(jnp.int32, (tt, tt), 0)
    c = lax.broadcasted_iota(jnp.int32, (tt, tt), 1)
    tri = (c <= r) & ((r // grp) == (c // grp))
    lmat = jnp.where(tri, 1.0, 0.0).astype(BF16)
    hi = lf.astype(BF16)
    rem = lf - hi.astype(F32)
    mid = rem.astype(BF16)
    lo = (rem - mid.astype(F32)).astype(BF16)
    cs = _dot(lmat, lo) + _dot(lmat, mid) + _dot(lmat, hi) + carry_ref[0:1, :]
    carry_ref[...] = jnp.broadcast_to(cs[tt - 1:tt, :], carry_ref.shape)
    crow_ref[0] = cs.T[0:8, :]


def _gate_cumsum(small3, b_pad, grp):
    b, t, _ = small3.shape
    tt = _tile(t, 512)
    return pl.pallas_call(
        functools.partial(_gate_kernel, tt=tt, grp=grp),
        grid=(b, t // tt),
        in_specs=[pl.BlockSpec((1, tt, LANES), lambda i, j: (i, j, 0)),
                  pl.BlockSpec((1, LANES), lambda i, j: (0, 0))],
        out_specs=[pl.BlockSpec((1, tt, LANES), lambda i, j: (i, j, 0)),
                   pl.BlockSpec((1, 8, tt), lambda i, j: (i, 0, j))],
        out_shape=[jax.ShapeDtypeStruct((b, t, LANES), F32), jax.ShapeDtypeStruct((b, 8, t), F32)],
        scratch_shapes=[pltpu.VMEM((8, LANES), F32)],
        compiler_params=_cparams(("parallel", "arbitrary")),
        name="fox_gate_cumsum",
    )(small3, b_pad)


def _conv_kernel(glu_ref, st_ref, dw_ref, g_ref, b_ref, pw_ref, o_ref, so_ref, ext_ref, *, tm):
    j = pl.program_id(1)

    @pl.when(j == 0)
    def _():
        ext_ref[0:CONV_CARRY, :] = st_ref[0]

    glu = glu_ref[0]
    u = glu[:, :CONV_CH] * _sigmoid(glu[:, CONV_CH:])
    ext_ref[CONV_CARRY:CONV_CARRY + tm, :] = u
    off = CONV_CARRY - (CONV_WIDTH - 1)
    acc = jnp.zeros((tm, CONV_CH), F32)
    for k in range(CONV_WIDTH):
        acc = acc + dw_ref[k:k + 1, :] * ext_ref[pl.ds(off + k, tm), :]
    y = _layer_norm(acc, g_ref[...], b_ref[...])
    y = y * _sigmoid(y)
    o_ref[0] = _dot(y.astype(BF16), pw_ref[...])
    last = ext_ref[tm:tm + CONV_CARRY, :]
    so_ref[0] = last
    ext_ref[0:CONV_CARRY, :] = last


def _conv_module(glu3, state, dw, g, b, pw):
    bsz, t, _ = glu3.shape
    tm = _tile(t, 512)
    return pl.pallas_call(
        functools.partial(_conv_kernel, tm=tm),
        grid=(bsz, t // tm),
        in_specs=[pl.BlockSpec((1, tm, 2 * CONV_CH), lambda i, j: (i, j, 0)),
                  pl.BlockSpec((1, CONV_CARRY, CONV_CH), lambda i, j: (i, 0, 0)),
                  _resident(dw.shape), _resident(g.shape), _resident(b.shape), _resident(pw.shape)],
        out_specs=[pl.BlockSpec((1, tm, CONV_CH), lambda i, j: (i, j, 0)),
                   pl.BlockSpec((1, CONV_CARRY, CONV_CH), lambda i, j: (i, 0, 0))],
        out_shape=[jax.ShapeDtypeStruct((bsz, t, CONV_CH), F32),
                   jax.ShapeDtypeStruct((bsz, CONV_CARRY, CONV_CH), F32)],
        scratch_shapes=[pltpu.VMEM((tm + CONV_CARRY, CONV_CH), F32)],
        compiler_params=_cparams(("parallel", "arbitrary")),
        name="conv_module",
    )(glu3, state, dw, g, b, pw)


ROW_CHUNK = 32


def _flash_tile(s_ref, p_ref, m_ref, l_ref, acc_ref, v, bias_fn):
    rows, tk = s_ref.shape
    rc = min(ROW_CHUNK, rows)
    for i in range(rows // rc):
        r0 = i * rc
        sl = pl.ds(r0, rc)
        sc = bias_fn(s_ref[sl, :], r0)
        m_prev = m_ref[sl, :]
        m_new = jnp.maximum(m_prev, jnp.max(sc, axis=1, keepdims=True))
        alpha = jnp.exp(m_prev - m_new)
        p = jnp.exp(sc - jnp.tile(m_new, (1, tk // LANES)))
        l_ref[sl, :] = alpha * l_ref[sl, :] + jnp.sum(p, axis=1, keepdims=True)
        acc_ref[sl, :] = alpha * acc_ref[sl, :]
        m_ref[sl, :] = m_new
        p_ref[sl, :] = p.astype(BF16)
    acc_ref[...] += _dot(p_ref[...], v)


def _fox_kernel(q_ref, k_ref, v_ref, c_ref, o_ref, s_ref, p_ref, m_ref, l_ref, acc_ref, *, tq):
    qi = pl.program_id(2)
    _osm_init(m_ref, l_ref, acc_ref)
    q = q_ref[0]
    qs = jnp.concatenate([q[:, :LANES], q[:, LANES:]], axis=0)

    def tile(ki, diag):
        k0 = pl.multiple_of(ki * tq, tq)
        s_ref[...] = _nt(qs, k_ref[0, pl.ds(k0, tq), :])
        c2 = c_ref[0, 0, 0:2, pl.ds(k0, tq)]

        def bias(sc, r0):
            sc = sc - (c2[0:1, :] if r0 < tq else c2[1:2, :])
            if diag:
                col = lax.broadcasted_iota(jnp.int32, sc.shape, 1)
                row = lax.broadcasted_iota(jnp.int32, sc.shape, 0) + r0 % tq
                sc = jnp.where(col <= row, sc, -jnp.inf)
            return sc

        _flash_tile(s_ref, p_ref, m_ref, l_ref, acc_ref, v_ref[0, pl.ds(k0, tq), :], bias)

    def far(ki, carry):
        tile(ki, False)
        return carry

    lax.fori_loop(0, qi, far, 0)
    tile(qi, True)
    o = acc_ref[...] / l_ref[...]
    o_ref[0, :, :LANES] = o[:tq]
    o_ref[0, :, LANES:] = o[tq:]


def _fox_prompt(fq, fkb, fvb, cpair):
    b, t, _ = fq.shape
    tq = _tile(t, 512)
    rows = 2 * tq
    return pl.pallas_call(
        functools.partial(_fox_kernel, tq=tq),
        grid=(b, FOX_HEADS // 2, t // tq),
        in_specs=[pl.BlockSpec((1, tq, 2 * LANES), lambda i, p, j: (i, j, p)),
                  pl.BlockSpec((1, t, LANES), lambda i, p, j: (i, 0, p)),
                  pl.BlockSpec((1, t, LANES), lambda i, p, j: (i, 0, p)),
                  pl.BlockSpec((1, 1, 8, t), lambda i, p, j: (i, p, 0, 0))],
        out_specs=pl.BlockSpec((1, tq, 2 * LANES), lambda i, p, j: (i, j, p)),
        out_shape=jax.ShapeDtypeStruct((b, t, WIDE), F32),
        scratch_shapes=[pltpu.VMEM((rows, tq), F32), pltpu.VMEM((rows, tq), BF16),
                        pltpu.VMEM((rows, LANES), F32), pltpu.VMEM((rows, LANES), F32),
                        pltpu.VMEM((rows, LANES), F32)],
        compiler_params=_cparams(("parallel", "parallel", "arbitrary")),
        name="fox_prompt",
    )(fq, fkb, fvb, cpair)


def _bucket_thresholds():
    d = np.arange(4 * MAX_DISTANCE)
    max_exact = N_BUCKETS // 2
    df = np.maximum(d, 1).astype(np.float32)
    log_b = max_exact + (np.log(df / np.float32(max_exact)) / np.float32(math.log(MAX_DISTANCE / max_exact))
                         * np.float32(N_BUCKETS - max_exact)).astype(np.int32)
    bucket = np.where(d < max_exact, d, np.minimum(log_b, N_BUCKETS - 1))
    return [int(np.argmax(bucket >= bk)) for bk in range(N_BUCKETS)]


def _bias_kernel(rb_ref, o_ref, *, q0, a, b, maxd, sub_far, tqb, rows, thr):
    h = pl.program_id(0)
    blk = pl.program_id(1)
    nk = o_ref.shape[2]
    far = rb_ref[(N_BUCKETS - 1) * NSA_HEADS + h]

    def body(r, carry):
        i = lax.broadcasted_iota(jnp.int32, (rows, nk), 0) + (q0 + blk * tqb + r * rows)
        n = lax.broadcasted_iota(jnp.int32, (rows, nk), 1)
        dist = i - (a * n + b)
        val = jnp.full((rows, nk), rb_ref[h], F32)
        for bk in range(1, N_BUCKETS):
            val = jnp.where(dist >= thr[bk], rb_ref[bk * NSA_HEADS + h], val)
        if sub_far:
            val = val - far
        ok = dist >= 0
        if maxd is not None:
            ok = ok & (dist <= maxd)
        o_ref[0, pl.ds(pl.multiple_of(r * rows, rows), rows), :] = jnp.where(ok, val, NEG)
        return carry

    lax.fori_loop(0, tqb // rows, body, 0)


def _bias_table(rb_flat, nq, nk, *, q0, a, b, maxd=None, sub_far=False):
    tqb = _tile(nq, 128)
    rows = _tile(tqb, 32)
    return pl.pallas_call(
        functools.partial(_bias_kernel, q0=q0, a=a, b=b, maxd=maxd, sub_far=sub_far, tqb=tqb, rows=rows,
                          thr=_bucket_thresholds()),
        grid=(NSA_HEADS, nq // tqb),
        in_specs=[pl.BlockSpec(memory_space=pltpu.SMEM)],
        out_specs=pl.BlockSpec((1, tqb, nk), lambda h, i: (h, i, 0)),
        out_shape=jax.ShapeDtypeStruct((NSA_HEADS, nq, nk), F32),
        compiler_params=_cparams(("parallel", "arbitrary")),
        name="t5_bias_table",
    )(rb_flat)


def _compress_finish(lead, tail, tail_next, o_ref):
    nc = lead.shape[0]
    shifted = pltpu.roll(tail, nc - 1, 0)
    row = lax.broadcasted_iota(jnp.int32, lead.shape, 0)
    o_ref[0, 0] = (lead + jnp.where(row == nc - 1, tail_next, shifted)).astype(BF16)


def _compress_kernel(x_ref, nx_ref, w_ref, o_ref, *, nc):
    j = pl.program_id(2)
    lead = jnp.zeros((nc, LANES), F32)
    tail = jnp.zeros((nc, LANES), F32)
    tail_next = jnp.zeros((1, LANES), F32)
    for r in range(CMP_STRIDE):
        rows = x_ref[0, pl.ds(r, nc, stride=CMP_STRIDE), :]
        lead = lead + rows * w_ref[r:r + 1, :]
        tail = tail + rows * w_ref[CMP_STRIDE + r:CMP_STRIDE + r + 1, :]
        tail_next = tail_next + nx_ref[0, r:r + 1, :] * w_ref[CMP_STRIDE + r:CMP_STRIDE + r + 1, :]
    tail_next = jnp.where(j == pl.num_programs(2) - 1, 0.0, tail_next)
    _compress_finish(lead, tail, tail_next, o_ref)


def _compress(cmp3, w):
    b, t, _ = cmp3.shape
    tc = _tile(t, 2048)
    nc = tc // CMP_STRIDE
    nblk = t // tc
    sub = tc // CMP_STRIDE
    return pl.pallas_call(
        functools.partial(_compress_kernel, nc=nc),
        grid=(b, 2, nblk),
        in_specs=[pl.BlockSpec((1, tc, LANES), lambda i, h, j: (i, j, h)),
                  pl.BlockSpec((1, CMP_STRIDE, LANES),
                               lambda i, h, j: (i, jnp.minimum((j + 1) * sub, nblk * sub - 1), h)),
                  pl.BlockSpec((CMP_BLOCK, LANES), lambda i, h, j: (0, h))],
        out_specs=pl.BlockSpec((1, 1, nc, LANES), lambda i, h, j: (i, h, j, 0)),
        out_shape=jax.ShapeDtypeStruct((b, 2, t // CMP_STRIDE, LANES), BF16),
        compiler_params=_cparams(("parallel", "parallel", "arbitrary")),
        name="nsa_compress",
    )(cmp3, cmp3, w)


def _compress_paged_kernel(pt_ref, *refs, npg):
    pages = refs[:npg]
    nx_ref, w_ref, o_ref, x_ref = refs[npg:]
    j = pl.program_id(2)
    nc = npg * PAGE // CMP_STRIDE
    for jp, pg in enumerate(pages):
        x_ref[jp * PAGE:(jp + 1) * PAGE, :] = pg[0, 0].T
    nxt = nx_ref[0, 0].T
    lead = jnp.zeros((nc, LANES), F32)
    tail = jnp.zeros((nc, LANES), F32)
    tail_next = jnp.zeros((1, LANES), F32)
    for r in range(CMP_STRIDE):
        rows = x_ref[pl.ds(r, nc, stride=CMP_STRIDE), :]
        lead = lead + rows * w_ref[r:r + 1, :]
        tail = tail + rows * w_ref[CMP_STRIDE + r:CMP_STRIDE + r + 1, :]
        tail_next = tail_next + nxt[r:r + 1, :] * w_ref[CMP_STRIDE + r:CMP_STRIDE + r + 1, :]
    tail_next = jnp.where(j == pl.num_programs(2) - 1, 0.0, tail_next)
    _compress_finish(lead, tail, tail_next, o_ref)


def _page_map(b, i, pt, *, j, layer, npg, last):
    return (pt[b, jnp.minimum(i * npg + j, last)], layer, 0, 0)


def _page_half_map(b, h, i, pt, *, j, layer, npg, last):
    return (pt[b, jnp.minimum(i * npg + j, last)], layer, h, 0)


def _compress_paged(cache4, page_table, layer, w):
    b, n_pages = page_table.shape
    npg = 2 * PAGES_PER_STEP if n_pages % (2 * PAGES_PER_STEP) == 0 else PAGES_PER_STEP
    nsteps = n_pages // npg
    nc = npg * PAGE // CMP_STRIDE
    pm = functools.partial(_page_half_map, layer=layer, npg=npg, last=n_pages - 1)
    in_specs = [pl.BlockSpec((1, 1, LANES, PAGE), functools.partial(pm, j=j)) for j in range(npg + 1)]
    in_specs.append(pl.BlockSpec((CMP_BLOCK, LANES), lambda i, h, j, pt: (0, h)))
    grid_spec = pltpu.PrefetchScalarGridSpec(
        num_scalar_prefetch=1, grid=(b, 2, nsteps), in_specs=in_specs,
        out_specs=pl.BlockSpec((1, 1, nc, LANES), lambda i, h, j, pt: (i, h, j, 0)),
        scratch_shapes=[pltpu.VMEM((npg * PAGE, LANES), F32)])
    return pl.pallas_call(
        functools.partial(_compress_paged_kernel, npg=npg),
        grid_spec=grid_spec,
        out_shape=jax.ShapeDtypeStruct((b, 2, n_pages * PAGE // CMP_STRIDE, LANES), BF16),
        compiler_params=_cparams(("parallel", "parallel", "arbitrary")),
        name="nsa_compress_paged",
    )(page_table, *([cache4] * (npg + 1)), w)


def _cmp_kernel(q_ref, kc_ref, vc_ref, tb_ref, a_ref, o_ref, sel_ref, *, tq, q_start, ns, topk):
    qb = pl.program_id(2)
    q = q_ref[0]
    q3 = jnp.concatenate([q[:, r * LANES:(r + 1) * LANES] for r in range(NSA_REP)], axis=0)
    ncp = kc_ref.shape[2]
    tb = tb_ref[...].reshape(NSA_REP * tq, ncp)
    s = _nt(q3, kc_ref[0, 0]) + tb
    valid = tb > NEG_TEST
    m = jnp.max(s, axis=1, keepdims=True)
    e = jnp.where(valid, jnp.exp(s - m), 0.0)
    p = e / jnp.maximum(jnp.sum(e, axis=1, keepdims=True), 1e-30)
    o = _dot(p.astype(BF16), vc_ref[0, 0])
    for r in range(NSA_REP):
        o_ref[0, :, r * LANES:(r + 1) * LANES] = o[r * tq:(r + 1) * tq]
    imp_p = p[0:tq] + p[tq:2 * tq] + p[2 * tq:3 * tq]
    nsp = a_ref.shape[0]
    tr = tq % LANES == 0
    hi = imp_p.astype(BF16)
    rem = imp_p - hi.astype(F32)
    mid = rem.astype(BF16)
    lo = (rem - mid.astype(F32)).astype(BF16)
    amat = a_ref[...]
    if tr:
        imp = _nt(amat, lo) + _nt(amat, mid) + _nt(amat, hi)
        shape, jdim, ax = (nsp, tq), 0, 0
    else:
        imp = _nt(lo, amat) + _nt(mid, amat) + _nt(hi, amat)
        shape, jdim, ax = (tq, nsp), 1, 1
    jj = lax.broadcasted_iota(jnp.int32, shape, jdim)
    t = lax.broadcasted_iota(jnp.int32, shape, 1 - jdim) + (q_start + qb * tq)
    cur = t // SEL_BLOCK
    forced = (jj == 0) | (jj == cur) | (jj == cur - 1)
    score = jnp.where(forced, 1e9, jnp.where(jj <= cur, imp, -1e9))
    score = jnp.where(jj < ns, score, -jnp.inf)
    jf = jj.astype(F32)
    chosen = jnp.zeros(shape, F32)
    for _ in range(topk):
        mx = jnp.max(score, axis=ax, keepdims=True)
        first = jnp.min(jnp.where(score == mx, jf, 1e9), axis=ax, keepdims=True)
        hit = jf == first
        chosen = jnp.where(hit, 1.0, chosen)
        score = jnp.where(hit, -jnp.inf, score)
    selb = jnp.where(chosen > 0.5, 0.0, SEL_NEG)
    sel_ref[0, 0] = (selb.T if tr else selb).astype(BF16)


def _importance_matrix(ncp, nc, nsp):
    n = np.arange(ncp)[None, :]
    j = np.arange(nsp)[:, None]
    a = (n >= SEL_PER_CMP * j - 1) & (n <= SEL_PER_CMP * j + SEL_PER_CMP - 1) & (n < nc)
    return jnp.asarray(a, BF16)


def _nsa_cmp(nq, kvc, table, *, q_start, tk_total):
    b, t, _ = nq.shape
    ncp = kvc.shape[2]
    nc = -(-tk_total // CMP_STRIDE) - 1
    ns = -(-tk_total // SEL_BLOCK)
    nsp = -(-ns // LANES) * LANES
    tq = _tile(t, 128)
    amat = _importance_matrix(ncp, nc, nsp)
    g3 = NSA_REP * LANES
    return pl.pallas_call(
        functools.partial(_cmp_kernel, tq=tq, q_start=q_start, ns=ns, topk=min(SEL_TOPK, ns)),
        grid=(b, NSA_KV_HEADS, t // tq),
        in_specs=[pl.BlockSpec((1, tq, g3), lambda i, g, j: (i, j, g)),
                  pl.BlockSpec((1, 1, ncp, LANES), lambda i, g, j: (i, 0, 0, 0)),
                  pl.BlockSpec((1, 1, ncp, LANES), lambda i, g, j: (i, 1, 0, 0)),
                  pl.BlockSpec((NSA_REP, tq, ncp), lambda i, g, j: (g, j, 0)),
                  _resident(amat.shape)],
        out_specs=[pl.BlockSpec((1, tq, g3), lambda i, g, j: (i, j, g)),
                   pl.BlockSpec((1, 1, tq, nsp), lambda i, g, j: (i, g, j, 0))],
        out_shape=[jax.ShapeDtypeStruct((b, t, WIDE), F32),
                   jax.ShapeDtypeStruct((b, NSA_KV_HEADS, t, nsp), BF16)],
        compiler_params=_cparams(("parallel", "parallel", "arbitrary")),
        name="nsa_compressed_select",
    )(nq, kvc, kvc, table, amat)


def _sel_kernel(q_ref, sb_ref, k_ref, v_ref, oh_ref, nb_ref, o_ref,
                lhs_ref, s_ref, p_ref, m_ref, l_ref, acc_ref, *, tq):
    qi = pl.program_id(2)
    _osm_init(m_ref, l_ref, acc_ref)
    q = q_ref[0]
    lhs_ref[:, :LANES] = jnp.concatenate([q[:, r * LANES:(r + 1) * LANES] for r in range(NSA_REP)], axis=0)
    lhs_ref[:, LANES:] = jnp.concatenate([sb_ref[0, 0]] * NSA_REP, axis=0)

    def tile(ki, mode):
        k0 = pl.multiple_of(ki * tq, tq)
        kp = jnp.concatenate([k_ref[0, pl.ds(k0, tq), :], oh_ref[pl.ds(k0, tq), :]], axis=1)
        s_ref[...] = _nt(lhs_ref[...], kp)

        def bias(sc, r0):
            if mode == "far":
                return sc
            head = r0 // tq
            rows = pl.ds(r0 % tq, sc.shape[0])
            if mode == "near":
                return sc + nb_ref[head, rows, 0:tq]
            tb = nb_ref[head, rows, tq:2 * tq]
            return jnp.where(tb > NEG_TEST, sc + tb, -jnp.inf)

        _flash_tile(s_ref, p_ref, m_ref, l_ref, acc_ref, v_ref[0, pl.ds(k0, tq), :], bias)

    def far(ki, carry):
        tile(ki, "far")
        return carry

    lax.fori_loop(0, jnp.maximum(qi - 1, 0), far, 0)

    @pl.when(qi >= 1)
    def _():
        tile(qi - 1, "near")

    tile(qi, "diag")
    o = acc_ref[...] / l_ref[...]
    for r in range(NSA_REP):
        o_ref[0, :, r * LANES:(r + 1) * LANES] = o[r * tq:(r + 1) * tq]


def _nsa_sel_prompt(nq, selbias, selkv, onehot, near):
    b, t, _ = nq.shape
    tq = near.shape[1]
    g3 = NSA_REP * LANES
    rows = NSA_REP * tq
    return pl.pallas_call(
        functools.partial(_sel_kernel, tq=tq),
        grid=(b, NSA_KV_HEADS, t // tq),
        in_specs=[pl.BlockSpec((1, tq, g3), lambda i, g, j: (i, j, g)),
                  pl.BlockSpec((1, 1, tq, LANES), lambda i, g, j: (i, g, j, 0)),
                  pl.BlockSpec((1, t, LANES), lambda i, g, j: (i, 0, 0)),
                  pl.BlockSpec((1, t, LANES), lambda i, g, j: (i, 0, 1)),
                  _resident(onehot.shape),
                  pl.BlockSpec((NSA_REP, tq, 2 * tq), lambda i, g, j: (g, 0, 0))],
        out_specs=pl.BlockSpec((1, tq, g3), lambda i, g, j: (i, j, g)),
        out_shape=jax.ShapeDtypeStruct((b, t, WIDE), F32),
        scratch_shapes=[pltpu.VMEM((rows, 2 * LANES), BF16),
                        pltpu.VMEM((rows, tq), F32), pltpu.VMEM((rows, tq), BF16),
                        pltpu.VMEM((rows, LANES), F32), pltpu.VMEM((rows, LANES), F32),
                        pltpu.VMEM((rows, LANES), F32)],
        compiler_params=_cparams(("parallel", "parallel", "arbitrary")),
        name="nsa_selected_prompt",
    )(nq, selbias, selkv, selkv, onehot, near)


def _win_kernel(q_ref, *refs, nkb, tq, front):
    k_refs = refs[:nkb]
    v_refs = refs[nkb:2 * nkb]
    tb_ref, o_ref = refs[2 * nkb:]
    qb = pl.program_id(2)
    q = q_ref[0]
    q3 = jnp.concatenate([q[:, r * LANES:(r + 1) * LANES] for r in range(NSA_REP)], axis=0)
    k = jnp.concatenate([r[0] for r in k_refs], axis=0) if nkb > 1 else k_refs[0][0]
    v = jnp.concatenate([r[0] for r in v_refs], axis=0) if nkb > 1 else v_refs[0][0]
    nk = k.shape[0]
    tb = tb_ref[...].reshape(NSA_REP * tq, nk)
    s = _nt(q3, k) + tb
    kl = lax.broadcasted_iota(jnp.int32, s.shape, 1) + (qb * tq - front)
    s = jnp.where((tb > NEG_TEST) & (kl >= 0), s, -jnp.inf)
    m = jnp.max(s, axis=1, keepdims=True)
    e = jnp.exp(s - m)
    p = e / jnp.sum(e, axis=1, keepdims=True)
    o = _dot(p.astype(BF16), v)
    for r in range(NSA_REP):
        o_ref[0, :, r * LANES:(r + 1) * LANES] = o[r * tq:(r + 1) * tq]


def _nsa_win(nq, winkv, table, *, tq, kblk, nkb, front):
    b, t, _ = nq.shape
    g3 = NSA_REP * LANES
    in_specs = [pl.BlockSpec((1, tq, g3), lambda i, g, j: (i, j, g))]
    for half in range(2):
        for jb in range(nkb):
            in_specs.append(pl.BlockSpec((1, kblk, LANES), functools.partial(
                lambda i, g, j, jb, half: (i, j + jb, half), jb=jb, half=half)))
    in_specs.append(pl.BlockSpec((NSA_REP, tq, nkb * kblk), lambda i, g, j: (g, 0, 0)))
    return pl.pallas_call(
        functools.partial(_win_kernel, nkb=nkb, tq=tq, front=front),
        grid=(b, NSA_KV_HEADS, t // tq),
        in_specs=in_specs,
        out_specs=pl.BlockSpec((1, tq, g3), lambda i, g, j: (i, j, g)),
        out_shape=jax.ShapeDtypeStruct((b, t, WIDE), F32),
        compiler_params=_cparams(("parallel", "parallel", "arbitrary")),
        name="nsa_window",
    )(nq, *([winkv] * (2 * nkb)), table)


def _out_kernel(oa_ref, of_ref, oc_ref, os_ref, ow_ref, sm_ref, x_ref, w_ref, g_ref, b_ref, o_ref, *, alpha):
    sg = _sigmoid(sm_ref[...])
    parts = [oa_ref[...].astype(BF16), of_ref[...].astype(BF16)]
    for h in range(NSA_HEADS):
        c0 = SM_NG + 3 * h
        lo, hi = h * LANES, (h + 1) * LANES
        o = (sg[:, c0:c0 + 1] * oc_ref[:, lo:hi] + sg[:, c0 + 1:c0 + 2] * os_ref[:, lo:hi]
             + sg[:, c0 + 2:c0 + 3] * ow_ref[:, lo:hi])
        parts.append(o.astype(BF16))
    mixed = jnp.concatenate(parts, axis=1)
    y = alpha * x_ref[...] + _dot(mixed, w_ref[...])
    o_ref[...] = _layer_norm(y, g_ref[...], b_ref[...])


def _out_ln(oa, of, oc, os_, ow, small, x2, w, g, b, alpha):
    m, d = x2.shape
    tm = _tile(m, 256)
    row = lambda width: pl.BlockSpec((tm, width), lambda i: (i, 0))
    return pl.pallas_call(
        functools.partial(_out_kernel, alpha=alpha),
        grid=(m // tm,),
        in_specs=[row(CONV_CH), row(WIDE), row(WIDE), row(WIDE), row(WIDE), row(LANES), row(d),
                  _resident(w.shape), _resident(g.shape), _resident(b.shape)],
        out_specs=row(d),
        out_shape=jax.ShapeDtypeStruct((m, d), F32),
        compiler_params=_cparams(("parallel",)),
        name="out_proj_ln",
    )(oa, of, oc, os_, ow, small, x2, w, g, b)


def _ffn_kernel(x_ref, st_ref, wi_ref, wo_ref, dw_ref, g_ref, b_ref, o_ref, so_ref, carry_ref, ext_ref,
                *, tm, dff, fc, alpha):
    j = pl.program_id(1)

    @pl.when(j == 0)
    def _():
        carry_ref[...] = st_ref[0]

    x = x_ref[0]
    xb = x.astype(BF16)
    acc = jnp.zeros(x.shape, F32)
    for c in range(dff // fc):
        lo, hi = c * fc, (c + 1) * fc
        up = _dot(xb, wi_ref[:, lo:hi])
        gate = _dot(xb, wi_ref[:, dff + lo:dff + hi])
        ext_ref[0:FFN_CARRY, :] = carry_ref[:, lo:hi]
        ext_ref[FFN_CARRY:FFN_CARRY + tm, :] = up
        y = (dw_ref[2:3, lo:hi] * up + dw_ref[1:2, lo:hi] * ext_ref[pl.ds(FFN_CARRY - 1, tm), :]
             + dw_ref[0:1, lo:hi] * ext_ref[pl.ds(FFN_CARRY - 2, tm), :])
        carry_ref[:, lo:hi] = ext_ref[tm:tm + FFN_CARRY, :]
        gelu = 0.5 * y * (1.0 + jnp.tanh(math.sqrt(2.0 / math.pi) * (y + 0.044715 * (y * y * y))))
        acc = acc + _dot((gelu * gate).astype(BF16), wo_ref[lo:hi, :])
    so_ref[0] = carry_ref[...]
    o_ref[0] = _layer_norm(alpha * x + acc, g_ref[...], b_ref[...])


def _ffn(x3, state, wi, wo, dw, g, b, alpha):
    bsz, t, d = x3.shape
    dff = wo.shape[0]
    tm = _tile(t, 256)
    fc = 256
    return pl.pallas_call(
        functools.partial(_ffn_kernel, tm=tm, dff=dff, fc=fc, alpha=alpha),
        grid=(bsz, t // tm),
        in_specs=[pl.BlockSpec((1, tm, d), lambda i, j: (i, j, 0)),
                  pl.BlockSpec((1, FFN_CARRY, dff), lambda i, j: (i, 0, 0)),
                  _resident(wi.shape), _resident(wo.shape), _resident(dw.shape),
                  _resident(g.shape), _resident(b.shape)],
        out_specs=[pl.BlockSpec((1, tm, d), lambda i, j: (i, j, 0)),
                   pl.BlockSpec((1, FFN_CARRY, dff), lambda i, j: (i, 0, 0))],
        out_shape=[jax.ShapeDtypeStruct((bsz, t, d), F32), jax.ShapeDtypeStruct((bsz, FFN_CARRY, dff), F32)],
        scratch_shapes=[pltpu.VMEM((FFN_CARRY, dff), F32), pltpu.VMEM((tm + FFN_CARRY, fc), F32)],
        compiler_params=_cparams(("parallel", "arbitrary"), vmem_mb=56),
        name="conv_ffn_ln",
    )(x3, state, wi, wo, dw, g, b)


def _fox_dec_kernel(pt_ref, q_ref, u_ref, kn_ref, vn_ref, cn_ref, *refs, npg):
    kp = refs[:npg]
    vp = refs[npg:2 * npg]
    lp = refs[2 * npg:3 * npg]
    o_ref, m_ref, l_ref, acc_ref, carry_ref = refs[3 * npg:]
    i = pl.program_id(1)

    @pl.when(i == 0)
    def _():
        _osm_init(m_ref, l_ref, acc_ref)
        carry_ref[...] = jnp.zeros(carry_ref.shape, F32)

    q = q_ref[0]
    lf = jnp.concatenate([r[0, 0] for r in lp], axis=0)
    cw = _dot3(lf, u_ref[...])
    carry = carry_ref[...]
    biases = []
    for j in range(npg):
        cj = cw[8 * j:8 * j + 8] + carry
        carry = carry + cw[8 * j:8 * j + 8, LANES - 1:LANES]
        biases.append(jnp.concatenate([cj] * 8, axis=0))
    carry_ref[...] = carry
    kt = jnp.concatenate([r[0, 0].astype(BF16) for r in kp], axis=1)
    vt = jnp.concatenate([r[0, 0].astype(BF16) for r in vp], axis=1)
    _osm(_dot(q, kt) - jnp.concatenate(biases, axis=1), vt, m_ref, l_ref, acc_ref, v_transposed=True)

    @pl.when(i == pl.num_programs(1) - 1)
    def _():
        sn = _nt(q, kn_ref[0].astype(BF16)) - jnp.concatenate([carry + cn_ref[0]] * 8, axis=0)
        col = lax.broadcasted_iota(jnp.int32, sn.shape, 1)
        qpos = lax.broadcasted_iota(jnp.int32, sn.shape, 0) // 8
        _osm(jnp.where(col <= qpos, sn, -jnp.inf), vn_ref[0].astype(BF16), m_ref, l_ref, acc_ref)
        o_ref[0] = acc_ref[...] / l_ref[:, :1]


def _fox_decode(qbd, cache_k4, cache_v4, lf_t, page_table, layer, knew, vnew, cnew):
    b, n_pages = page_table.shape
    npg = PAGES_PER_STEP
    pm = functools.partial(_page_map, layer=layer, npg=npg, last=n_pages - 1)
    umat = jnp.asarray(np.triu(np.ones((LANES, LANES), np.float32)), BF16)
    in_specs = [pl.BlockSpec((1, 64, FOX_W), lambda i, j, pt: (i, 0, 0)),
                _resident(umat.shape),
                pl.BlockSpec((1, PAGE, FOX_W), lambda i, j, pt: (i, 0, 0)),
                pl.BlockSpec((1, PAGE, FOX_W), lambda i, j, pt: (i, 0, 0)),
                pl.BlockSpec((1, 8, LANES), lambda i, j, pt: (i, 0, 0))]
    in_specs += [pl.BlockSpec((1, 1, FOX_W, PAGE), functools.partial(pm, j=j)) for j in range(npg)]
    in_specs += [pl.BlockSpec((1, 1, FOX_W, PAGE), functools.partial(pm, j=j)) for j in range(npg)]
    in_specs += [pl.BlockSpec((1, 1, 8, LANES), functools.partial(pm, j=j)) for j in range(npg)]
    grid_spec = pltpu.PrefetchScalarGridSpec(
        num_scalar_prefetch=1, grid=(b, n_pages // npg), in_specs=in_specs,
        out_specs=pl.BlockSpec((1, 64, FOX_W), lambda i, j, pt: (i, 0, 0)),
        scratch_shapes=[pltpu.VMEM((64, LANES), F32), pltpu.VMEM((64, LANES), F32),
                        pltpu.VMEM((64, FOX_W), F32), pltpu.VMEM((8, LANES), F32)])
    return pl.pallas_call(
        functools.partial(_fox_dec_kernel, npg=npg),
        grid_spec=grid_spec,
        out_shape=jax.ShapeDtypeStruct((b, 64, FOX_W), F32),
        compiler_params=_cparams(("parallel", "arbitrary")),
        name="fox_decode",
    )(page_table, qbd, umat, knew, vnew, cnew, *([cache_k4] * npg), *([cache_v4] * npg), *([lf_t] * npg))


def _sel_dec_kernel(pt_ref, q_ref, ss_ref, e_ref, tp_ref, tn_ref, sn_ref, kvn_ref, *refs, npg):
    pages = refs[:npg]
    o_ref, m_ref, l_ref, acc_ref = refs[npg:]
    i = pl.program_id(1)
    last = i == pl.num_programs(1) - 1

    @pl.when(i == 0)
    def _():
        _osm_init(m_ref, l_ref, acc_ref)

    q = q_ref[0]
    kt = jnp.concatenate([r[0, 0, :LANES, :].astype(BF16) for r in pages], axis=1)
    vt = jnp.concatenate([r[0, 0, LANES:, :].astype(BF16) for r in pages], axis=1)
    s = _dot(q, kt) + _dot(ss_ref[0, 0], e_ref[...]) + jnp.where(last, tp_ref[...], 0.0)
    _osm(s, vt, m_ref, l_ref, acc_ref, v_transposed=True)

    @pl.when(last)
    def _():
        kvn = kvn_ref[0]
        tn = tn_ref[...]
        sn = _nt(q, kvn[:, :LANES]) + tn + sn_ref[0]
        _osm(jnp.where(tn > NEG_TEST, sn, -jnp.inf), kvn[:, LANES:], m_ref, l_ref, acc_ref)
        o_ref[0] = acc_ref[...] / l_ref[...]


def _nsa_sel_decode(q48, selstep, emat, tpast, tnew, selnew, kvnew, cache4, page_table, layer):
    b, n_pages = page_table.shape
    npg = PAGES_PER_STEP
    pm = functools.partial(_page_map, layer=layer, npg=npg, last=n_pages - 1)
    rows = NSA_HEADS * 8
    in_specs = [pl.BlockSpec((1, rows, LANES), lambda i, j, pt: (i, 0, 0)),
                pl.BlockSpec((1, 1, rows, 2 * npg), lambda i, j, pt: (i, j, 0, 0)),
                _resident(emat.shape), _resident(tpast.shape), _resident(tnew.shape),
                pl.BlockSpec((1, rows, LANES), lambda i, j, pt: (i, 0, 0)),
                pl.BlockSpec((1, PAGE, 2 * LANES), lambda i, j, pt: (i, 0, 0))]
    in_specs += [pl.BlockSpec((1, 1, 2 * LANES, PAGE), functools.partial(pm, j=j)) for j in range(npg)]
    grid_spec = pltpu.PrefetchScalarGridSpec(
        num_scalar_prefetch=1, grid=(b, n_pages // npg), in_specs=in_specs,
        out_specs=pl.BlockSpec((1, rows, LANES), lambda i, j, pt: (i, 0, 0)),
        scratch_shapes=[pltpu.VMEM((rows, LANES), F32), pltpu.VMEM((rows, LANES), F32),
                        pltpu.VMEM((rows, LANES), F32)])
    return pl.pallas_call(
        functools.partial(_sel_dec_kernel, npg=npg),
        grid_spec=grid_spec,
        out_shape=jax.ShapeDtypeStruct((b, rows, LANES), F32),
        compiler_params=_cparams(("parallel", "arbitrary")),
        name="nsa_selected_decode",
    )(page_table, q48, selstep, emat, tpast, tnew, selnew, kvnew, *([cache4] * npg))


def _in_col_map():
    src = -np.ones(C_END, np.int64)
    scale = np.ones(C_END, np.float32)
    o_fq, o_fk, o_fv, o_ff = 2 * CONV_CH, 2 * CONV_CH + FOX_W, 2 * CONV_CH + 2 * FOX_W, 2 * CONV_CH + 3 * FOX_W
    o_nq = o_ff + FOX_HEADS
    o_nkv = o_nq + NSA_W
    o_ng = o_nkv + 6 * NSA_KV_W
    src[C_GLU:C_FQ] = np.arange(2 * CONV_CH)
    d = np.arange(HEAD_DIM)
    for h in range(FOX_HEADS):
        dst = C_FQ + LANES * h + HEAD_DIM * (h % 2)
        src[dst:dst + HEAD_DIM] = o_fq + HEAD_DIM * h + d
        scale[dst:dst + HEAD_DIM] = HEAD_DIM ** -0.5
    src[C_FK:C_FV] = o_fk + np.arange(FOX_W)
    src[C_FV:C_NQ] = o_fv + np.arange(FOX_W)
    for h in range(NSA_HEADS):
        dst = C_NQ + LANES * h + HEAD_DIM * (h // NSA_REP)
        src[dst:dst + HEAD_DIM] = o_nq + HEAD_DIM * h + d
        scale[dst:dst + HEAD_DIM] = HEAD_DIM ** -0.5
    src[C_CMP:C_SM] = o_nkv + np.arange(6 * NSA_KV_W)
    src[C_SM + SM_FF:C_SM + SM_FF + FOX_HEADS] = o_ff + np.arange(FOX_HEADS)
    src[C_SM + SM_NG:C_SM + SM_NG + 3 * NSA_HEADS] = o_ng + np.arange(3 * NSA_HEADS)
    return src, scale


def _out_row_map():
    n = CONV_CH + 2 * WIDE
    src = -np.ones(n, np.int64)
    src[:CONV_CH] = np.arange(CONV_CH)
    d = np.arange(HEAD_DIM)
    for h in range(FOX_HEADS):
        dst = CONV_CH + LANES * h + HEAD_DIM * (h % 2)
        src[dst:dst + HEAD_DIM] = CONV_CH + HEAD_DIM * h + d
    for h in range(NSA_HEADS):
        dst = CONV_CH + WIDE + LANES * h + HEAD_DIM * (h // NSA_REP)
        src[dst:dst + HEAD_DIM] = CONV_CH + FOX_W + HEAD_DIM * h + d
    return src


def _prep_layer(l, p):
    src, scale = _in_col_map()
    w_in = p['w_in'][l]
    wp = jnp.where(jnp.asarray(src >= 0)[None, :], jnp.take(w_in, jnp.asarray(np.maximum(src, 0)), axis=1), 0.0)
    wp = (wp * jnp.asarray(scale)[None, :]).astype(BF16)
    rsrc = _out_row_map()
    w_out = p['w_out'][l]
    wo = jnp.where(jnp.asarray(rsrc >= 0)[:, None], jnp.take(w_out, jnp.asarray(np.maximum(rsrc, 0)), axis=0), 0.0)
    dw = p['conv_dw'][l]
    fdw = p['ffn_dw'][l]
    return dict(
        wp=wp, wo=wo.astype(BF16),
        b_f=jnp.zeros((1, LANES), F32).at[0, SM_FF:SM_FF + FOX_HEADS].set(p['b_fgate'][l]),
        conv_dw=jnp.pad(dw, ((0, CONV_CARRY - dw.shape[0]), (0, 0))),
        conv_g=p['conv_ln_g'][l][None, :], conv_b=p['conv_ln_b'][l][None, :],
        conv_pw=p['conv_pw'][l].astype(BF16),
        cmp_w=jnp.transpose(p['nsa_cmp_w'][l], (1, 0, 2, 3)).reshape(CMP_BLOCK, 2 * NSA_KV_W),
        ln1_g=p['ln1_g'][l][None, :], ln1_b=p['ln1_b'][l][None, :],
        ffn_wi=p['ffn_w_in'][l].astype(BF16), ffn_wo=p['ffn_w_out'][l].astype(BF16),
        ffn_dw=jnp.pad(fdw, ((0, 8 - fdw.shape[0]), (0, 0))),
        ln2_g=p['ln2_g'][l][None, :], ln2_b=p['ln2_b'][l][None, :])


def _wide_from_heads(o, halves):
    z = jnp.zeros_like(o)
    sel = jnp.asarray(halves, jnp.int32)[:, None] == 0
    return jnp.concatenate([jnp.where(sel, o, z), jnp.where(sel, z, o)], axis=-1).reshape(o.shape[:-2] + (WIDE,))


def _layer_prompt(x, l, lw, tabs, alpha):
    b, t, d = x.shape
    m = b * t
    (glu, fq, fk, fv, fkb, fvb, nq, cmp_, sel, win, selb16, winb16, small) = _proj(x.reshape(m, d), lw['wp'])
    r3 = lambda a: a.reshape(b, t, a.shape[-1])
    out_a, conv_state = _conv_module(r3(glu), jnp.zeros((b, CONV_CARRY, CONV_CH), F32), lw['conv_dw'],
                                     lw['conv_g'], lw['conv_b'], lw['conv_pw'])
    lf, crow = _gate_cumsum(r3(small), lw['b_f'], t)
    cpair = jnp.pad(crow[:, :FOX_HEADS].reshape(b, FOX_HEADS // 2, 2, t), ((0, 0), (0, 0), (0, 6), (0, 0)))
    o_fox = _fox_prompt(r3(fq), r3(fkb), r3(fvb), cpair)
    kvc = _compress(r3(cmp_), lw['cmp_w'])
    nq3 = r3(nq)
    o_cmp, selbias = _nsa_cmp(nq3, kvc, tabs['cmp'], q_start=0, tk_total=t)
    o_sel = _nsa_sel_prompt(nq3, selbias, r3(selb16), tabs['onehot'], tabs['near'])
    tqw = tabs['win'].shape[1]
    winpad = jnp.pad(r3(winb16), ((0, 0), (WINDOW, 0), (0, 0)))
    o_win = _nsa_win(nq3, winpad, tabs['win'], tq=tqw, kblk=tqw, nkb=WINDOW // tqw + 1, front=WINDOW)
    x1 = _out_ln(out_a.reshape(m, CONV_CH), o_fox.reshape(m, WIDE), o_cmp.reshape(m, WIDE),
                 o_sel.reshape(m, WIDE), o_win.reshape(m, WIDE), small, x.reshape(m, d),
                 lw['wo'], lw['ln1_g'], lw['ln1_b'], alpha)
    dff = lw['ffn_wo'].shape[0]
    y, ffn_state = _ffn(x1.reshape(b, t, d), jnp.zeros((b, FFN_CARRY, dff), F32), lw['ffn_wi'], lw['ffn_wo'],
                        lw['ffn_dw'], lw['ln2_g'], lw['ln2_b'], alpha)
    buf = min(WINDOW, t)
    new = dict(fox_k=fk.reshape(b, t, FOX_HEADS, HEAD_DIM), fox_v=fv.reshape(b, t, FOX_HEADS, HEAD_DIM),
               fox_logf=lf[:, :, :FOX_HEADS],
               nsa_cmp=cmp_.reshape(b, t, 2, NSA_KV_HEADS, HEAD_DIM),
               nsa_sel=sel.reshape(b, t, 2, NSA_KV_HEADS, HEAD_DIM),
               nsa_win=r3(win)[:, t - buf:].reshape(b, buf, 2, NSA_KV_HEADS, HEAD_DIM),
               conv=conv_state[:, CONV_CARRY - (CONV_WIDTH - 1):],
               ffn_conv=ffn_state[:, FFN_CARRY - (FFN_CONV_WIDTH - 1):])
    return y, new


def _layer_sample(x, l, lw, tabs, alpha, cache):
    b, t, d = x.shape
    m = b * t
    pt = cache['page_table']
    n_pages = pt.shape[1]
    past = n_pages * PAGE
    (glu, fq, fk, fv, fkb, fvb, nq, cmp_, sel, win, selb16, winb16, small) = _proj(x.reshape(m, d), lw['wp'])
    r3 = lambda a: a.reshape(b, t, a.shape[-1])
    st = jnp.pad(cache['conv'][l], ((0, 0), (CONV_CARRY - (CONV_WIDTH - 1), 0), (0, 0)))
    out_a, conv_state = _conv_module(r3(glu), st, lw['conv_dw'], lw['conv_g'], lw['conv_b'], lw['conv_pw'])
    lf, crow = _gate_cumsum(small.reshape(1, m, LANES), lw['b_f'], t)
    lf = lf.reshape(b, t, LANES)
    cnew = jnp.pad(jnp.transpose(crow[0].reshape(8, b, t), (1, 0, 2)), ((0, 0), (0, 0), (0, LANES - t)))
    q6 = r3(fq).reshape(b, t, FOX_HEADS, 2, HEAD_DIM)
    qc = jnp.where((jnp.arange(FOX_HEADS) % 2 == 0)[:, None], q6[:, :, :, 0], q6[:, :, :, 1])
    eye = jnp.eye(8, FOX_HEADS, dtype=qc.dtype)
    qbd = (jnp.pad(qc, ((0, 0), (0, 0), (0, 2), (0, 0)))[:, :, :, None, :] * eye[None, None, :, :, None])
    qbd = qbd.reshape(b, t * 8, FOX_W)
    padrows = lambda a: jnp.pad(a, ((0, 0), (0, PAGE - t), (0, 0)))
    o64 = _fox_decode(qbd, cache['fox_k4'], cache['fox_v4'], cache['lf_t'], pt, l,
                      padrows(r3(fk)), padrows(r3(fv)), cnew)
    o5 = o64.reshape(b, t, 8, FOX_HEADS, HEAD_DIM)[:, :, :FOX_HEADS]
    o_f = jnp.moveaxis(jnp.diagonal(o5, axis1=2, axis2=3), -1, 2)
    o_fox = _wide_from_heads(o_f, [h % 2 for h in range(FOX_HEADS)])
    kvc = _compress_paged(cache['cmp4'], pt, l, lw['cmp_w'])
    nq3 = r3(nq)
    o_cmp, selbias = _nsa_cmp(nq3, kvc, tabs['cmp'], q_start=past, tk_total=past + t)
    npg = PAGES_PER_STEP
    nsteps = n_pages // npg
    sb = selbias[:, :, :, :2 * n_pages].reshape(b, NSA_KV_HEADS, 1, t, nsteps, 2 * npg)
    sb = jnp.broadcast_to(sb, (b, NSA_KV_HEADS, NSA_REP, t, nsteps, 2 * npg))
    selstep = jnp.transpose(sb, (0, 4, 1, 2, 3, 5)).reshape(b, nsteps, NSA_HEADS * t, 2 * npg)
    sn = selbias[:, :, :, 2 * n_pages].astype(F32)
    selnew = jnp.broadcast_to(sn[:, :, None, :, None], (b, NSA_KV_HEADS, NSA_REP, t, LANES))
    selnew = selnew.reshape(b, NSA_HEADS * t, LANES)
    q48 = jnp.transpose(nq3.reshape(b, t, NSA_HEADS, LANES), (0, 2, 1, 3)).reshape(b, NSA_HEADS * t, LANES)
    o48 = _nsa_sel_decode(q48, selstep, tabs['emat'], tabs['sel_past'], tabs['sel_new'], selnew,
                          padrows(r3(selb16)), cache['sel4'], pt, l)
    o_sel = jnp.transpose(o48.reshape(b, NSA_HEADS, t, LANES), (0, 2, 1, 3)).reshape(b, t, WIDE)
    win_all = jnp.concatenate([cache['nsa_win'][l].reshape(b, -1, 2 * NSA_KV_W), r3(win)], axis=1)
    wrows = tabs['win'].shape[2]
    winpad = jnp.pad(win_all.astype(BF16), ((0, 0), (0, wrows - win_all.shape[1]), (0, 0)))
    o_win = _nsa_win(nq3, winpad, tabs['win'], tq=t, kblk=wrows, nkb=1, front=0)
    x1 = _out_ln(out_a.reshape(m, CONV_CH), o_fox.reshape(m, WIDE), o_cmp.reshape(m, WIDE),
                 o_sel.reshape(m, WIDE), o_win.reshape(m, WIDE), small, x.reshape(m, d),
                 lw['wo'], lw['ln1_g'], lw['ln1_b'], alpha)
    fst = jnp.pad(cache['ffn_conv'][l], ((0, 0), (FFN_CARRY - (FFN_CONV_WIDTH - 1), 0), (0, 0)))
    y, ffn_state = _ffn(x1.reshape(b, t, d), fst, lw['ffn_wi'], lw['ffn_wo'], lw['ffn_dw'],
                        lw['ln2_g'], lw['ln2_b'], alpha)
    wlen = cache['nsa_win'].shape[2]
    new = dict(fox_k=fk.reshape(b, t, FOX_HEADS, HEAD_DIM), fox_v=fv.reshape(b, t, FOX_HEADS, HEAD_DIM),
               fox_logf=lf[:, :, :FOX_HEADS],
               nsa_cmp=cmp_.reshape(b, t, 2, NSA_KV_HEADS, HEAD_DIM),
               nsa_sel=sel.reshape(b, t, 2, NSA_KV_HEADS, HEAD_DIM),
               nsa_win=win_all[:, win_all.shape[1] - wlen:].reshape(b, wlen, 2, NSA_KV_HEADS, HEAD_DIM),
               conv=conv_state[:, CONV_CARRY - (CONV_WIDTH - 1):],
               ffn_conv=ffn_state[:, FFN_CARRY - (FFN_CONV_WIDTH - 1):])
    return y, new


ROW_KEYS = ('fox_k', 'fox_v', 'fox_logf', 'nsa_cmp', 'nsa_sel')
STATE_KEYS = ('nsa_win', 'conv', 'ffn_conv')


def _collect(news):
    out = {k: jnp.stack([n[k] for n in news], axis=1) for k in ROW_KEYS}
    out.update({k: jnp.stack([n[k] for n in news], axis=0) for k in STATE_KEYS})
    return out


def kernel(x_prompt, x_sample, cache_fox_k, cache_fox_v, cache_fox_logf, cache_nsa_cmp, cache_nsa_sel, state_nsa_win, state_conv, state_ffn_conv, page_table, w_in, b_fgate, conv_dw, conv_ln_g, conv_ln_b, conv_pw, nsa_cmp_w, rel_bias, w_out, ln1_g, ln1_b, ffn_w_in, ffn_dw, ffn_w_out, ln2_g, ln2_b):
    p = dict(w_in=w_in, b_fgate=b_fgate, conv_dw=conv_dw, conv_ln_g=conv_ln_g, conv_ln_b=conv_ln_b,
             conv_pw=conv_pw, nsa_cmp_w=nsa_cmp_w, rel_bias=rel_bias, w_out=w_out, ln1_g=ln1_g, ln1_b=ln1_b,
             ffn_w_in=ffn_w_in, ffn_dw=ffn_dw, ffn_w_out=ffn_w_out, ln2_g=ln2_g, ln2_b=ln2_b)
    depth = w_in.shape[0]
    alpha = (2 * depth) ** 0.25
    lws = [_prep_layer(l, p) for l in range(depth)]
    rb_flat = rel_bias.reshape(-1)

    b, t, _ = x_prompt.shape
    tq_sel = _tile(t, 512)
    tq_win = _tile(t, 256)
    tabs = dict(
        cmp=_bias_table(rb_flat, t, t // CMP_STRIDE, q0=0, a=CMP_STRIDE, b=CMP_BLOCK - 1),
        near=_bias_table(rb_flat, tq_sel, 2 * tq_sel, q0=0, a=1, b=-tq_sel, sub_far=True),
        win=_bias_table(rb_flat, tq_win, WINDOW + tq_win, q0=0, a=1, b=-WINDOW, maxd=WINDOW),
        onehot=jnp.asarray((np.arange(t)[:, None] // SEL_BLOCK) == np.arange(LANES)[None, :], BF16))
    x = x_prompt
    news = []
    for l in range(depth):
        x, new = _layer_prompt(x, l, lws[l], tabs, alpha)
        news.append(new)
    y_prompt, sp = x, _collect(news)

    db, dt, _ = x_sample.shape
    n_pool = cache_fox_k.shape[0]
    n_pages = page_table.shape[1]
    past = n_pages * PAGE
    npg = PAGES_PER_STEP
    wrows = -(-(state_nsa_win.shape[2] + dt) // LANES) * LANES
    near_s = _bias_table(rb_flat, dt, 2 * LANES, q0=0, a=1, b=-LANES, sub_far=True).reshape(NSA_HEADS * dt, 2 * LANES)
    tabs_s = dict(
        cmp=_bias_table(rb_flat, dt, past // CMP_STRIDE, q0=past, a=CMP_STRIDE, b=CMP_BLOCK - 1),
        win=_bias_table(rb_flat, dt, wrows, q0=state_nsa_win.shape[2], a=1, b=0, maxd=WINDOW),
        sel_past=jnp.pad(near_s[:, :LANES], ((0, 0), ((npg - 1) * PAGE, 0))),
        sel_new=near_s[:, LANES:],
        emat=jnp.asarray((np.arange(npg * PAGE)[None, :] // SEL_BLOCK) == np.arange(2 * npg)[:, None], BF16))
    lf_t = jnp.pad(jnp.swapaxes(cache_fox_logf, 2, 3), ((0, 0), (0, 0), (0, 8 - FOX_HEADS), (0, 0)))
    pos_minor = lambda c: jnp.moveaxis(c, 2, -1).reshape(n_pool, depth, -1, PAGE)
    cache = dict(
        page_table=page_table,
        fox_k4=pos_minor(cache_fox_k), fox_v4=pos_minor(cache_fox_v), lf_t=lf_t,
        cmp4=pos_minor(cache_nsa_cmp), sel4=pos_minor(cache_nsa_sel),
        nsa_win=state_nsa_win, conv=state_conv, ffn_conv=state_ffn_conv)
    x = x_sample
    news = []
    for l in range(depth):
        x, new = _layer_sample(x, l, lws[l], tabs_s, alpha, cache)
        news.append(new)
    y_sample, ss = x, _collect(news)

    return (y_prompt, y_sample,
            sp['fox_k'], sp['fox_v'], sp['fox_logf'], sp['nsa_cmp'], sp['nsa_sel'], sp['nsa_win'], sp['conv'], sp['ffn_conv'],
            ss['fox_k'], ss['fox_v'], ss['fox_logf'], ss['nsa_cmp'], ss['nsa_sel'], ss['nsa_win'], ss['conv'], ss['ffn_conv'])
```

```python
import functools
import math

import numpy as np
import jax
import jax.numpy as jnp
from jax import lax
from jax.experimental import pallas as pl
from jax.experimental.pallas import tpu as pltpu

F32 = jnp.float32
BF16 = jnp.bfloat16

HEAD_DIM = 64
LANES = 128
CONV_CH = 256
CONV_WIDTH = 31
CONV_CARRY = 32
FOX_HEADS = 6
NSA_HEADS = 6
NSA_KV_HEADS = 2
NSA_REP = NSA_HEADS // NSA_KV_HEADS
CMP_BLOCK = 32
CMP_STRIDE = 16
SEL_BLOCK = 64
SEL_TOPK = 16
SEL_PER_CMP = SEL_BLOCK // CMP_STRIDE
WINDOW = 512
N_BUCKETS = 32
MAX_DISTANCE = 128
FFN_CONV_WIDTH = 3
FFN_CARRY = 8
LN_EPS = 1e-5
PAGE = 128
PAGES_PER_STEP = 8

FOX_W = FOX_HEADS * HEAD_DIM
NSA_W = NSA_HEADS * HEAD_DIM
NSA_KV_W = NSA_KV_HEADS * HEAD_DIM
WIDE = 6 * LANES

NEG = -1e30
NEG_TEST = -1e29
SEL_NEG = -32768.0

C_GLU, C_FQ, C_FK, C_FV, C_NQ, C_CMP, C_SEL, C_WIN, C_SM, C_END = (
    0, 512, 1280, 1664, 2048, 2816, 3072, 3328, 3584, 3712)
SM_FF = 0
SM_NG = 6


def _cparams(sem, vmem_mb=48):
    return pltpu.CompilerParams(dimension_semantics=sem, vmem_limit_bytes=vmem_mb * 1024 * 1024)


def _dot(a, b):
    return jnp.dot(a, b, preferred_element_type=F32)


def _nt(a, b):
    return lax.dot_general(a, b, (((1,), (1,)), ((), ())), preferred_element_type=F32)


def _dot3(x, w):
    hi = x.astype(BF16)
    r = x - hi.astype(F32)
    mid = r.astype(BF16)
    lo = (r - mid.astype(F32)).astype(BF16)
    return _dot(lo, w) + _dot(mid, w) + _dot(hi, w)


def _resident(shape):
    return pl.BlockSpec(shape, lambda *a: (0,) * len(shape), pipeline_mode=pl.Buffered(1))


def _tile(n, pref):
    return pref if n % pref == 0 else n


def _sigmoid(x):
    return 1.0 / (1.0 + jnp.exp(-x))


def _layer_norm(y, g, b):
    mu = jnp.mean(y, axis=-1, keepdims=True)
    d = y - mu
    var = jnp.mean(d * d, axis=-1, keepdims=True)
    return d * lax.rsqrt(var + LN_EPS) * g + b


def _osm(s, v, m_ref, l_ref, acc_ref, v_transposed=False):
    m_prev = m_ref[:, :1]
    m_new = jnp.maximum(m_prev, jnp.max(s, axis=1, keepdims=True))
    alpha = jnp.exp(m_prev - m_new)
    p = jnp.exp(s - m_new)
    l_ref[...] = alpha * l_ref[...] + jnp.sum(p, axis=1, keepdims=True)
    pb = p.astype(BF16)
    acc_ref[...] = alpha * acc_ref[...] + (_nt(pb, v) if v_transposed else _dot(pb, v))
    m_ref[...] = jnp.broadcast_to(m_new, m_ref.shape)


def _osm_group(scores, values, stats, v_transposed=False):
    mids = []
    for s, (m_ref, l_ref, _) in zip(scores, stats):
        m_prev = m_ref[:, :1]
        m_new = jnp.maximum(m_prev, jnp.max(s, axis=1, keepdims=True))
        alpha = jnp.exp(m_prev - m_new)
        p = jnp.exp(s - m_new)
        l_ref[...] = alpha * l_ref[...] + jnp.sum(p, axis=1, keepdims=True)
        m_ref[...] = jnp.broadcast_to(m_new, m_ref.shape)
        mids.append((alpha, p.astype(BF16)))
    for (alpha, pb), v, (_, _, acc_ref) in zip(mids, values, stats):
        acc_ref[...] = alpha * acc_ref[...] + (_nt(pb, v) if v_transposed else _dot(pb, v))


def _osm_init(m_ref, l_ref, acc_ref):
    m_ref[...] = jnp.full(m_ref.shape, -jnp.inf, F32)
    l_ref[...] = jnp.zeros(l_ref.shape, F32)
    acc_ref[...] = jnp.zeros(acc_ref.shape, F32)


def _proj_kernel(x_ref, w_ref, glu, fq, fk, fv, fkb, fvb, nq, cmp_, sel, win, selb, winb, small):
    xb = x_ref[...].astype(BF16)

    def seg(a, b):
        return _dot(xb, w_ref[:, a:b])

    glu[...] = seg(C_GLU, C_FQ)
    fq[...] = seg(C_FQ, C_FK).astype(BF16)
    r = seg(C_FK, C_FV)
    fk[...] = r
    fkb[...] = r.astype(BF16)
    r = seg(C_FV, C_NQ)
    fv[...] = r
    fvb[...] = r.astype(BF16)
    nq[...] = seg(C_NQ, C_CMP).astype(BF16)
    cmp_[...] = seg(C_CMP, C_SEL)
    r = seg(C_SEL, C_WIN)
    sel[...] = r
    selb[...] = r.astype(BF16)
    r = seg(C_WIN, C_SM)
    win[...] = r
    winb[...] = r.astype(BF16)
    small[...] = seg(C_SM, C_END)


def _proj(x2, wp):
    m, d = x2.shape
    tm = _tile(m, 256)
    widths = [(C_FQ - C_GLU, F32), (C_FK - C_FQ, BF16), (FOX_W, F32), (FOX_W, F32), (FOX_W, BF16),
              (FOX_W, BF16), (C_CMP - C_NQ, BF16), (256, F32), (256, F32), (256, F32), (256, BF16),
              (256, BF16), (LANES, F32)]
    return pl.pallas_call(
        _proj_kernel,
        grid=(m // tm,),
        in_specs=[pl.BlockSpec((tm, d), lambda i: (i, 0)), _resident(wp.shape)],
        out_specs=[pl.BlockSpec((tm, w), lambda i: (i, 0)) for w, _ in widths],
        out_shape=[jax.ShapeDtypeStruct((m, w), dt) for w, dt in widths],
        compiler_params=_cparams(("parallel",)),
        name="in_proj",
    )(x2, wp)


def _gate_kernel(sm_ref, b_ref, lf_ref, crow_ref, carry_ref, *, tt, grp):
    j = pl.program_id(1)

    @pl.when((j == 0) | (grp < tt * pl.num_programs(1)))
    def _():
        carry_ref[...] = jnp.zeros(carry_ref.shape, F32)

    z = sm_ref[0] + b_ref[...]
    lf = jnp.minimum(z, 0.0) - jnp.log(1.0 + jnp.exp(-jnp.abs(z)))
    lane = lax.broadcasted_iota(jnp.int32, lf.shape, 1)
    lf = jnp.where(lane < FOX_HEADS, lf, 0.0)
    lf_ref[0] = lf
    r = lax.broadcasted_iota(jnp.int32, (tt, tt), 0)
    c = lax.broadcasted_iota(jnp.int32, (tt, tt), 1)
    tri = (c <= r) & ((r // grp) == (c // grp))
    lmat = jnp.where(tri, 1.0, 0.0).astype(BF16)
    hi = lf.astype(BF16)
    rem = lf - hi.astype(F32)
    mid = rem.astype(BF16)
    lo = (rem - mid.astype(F32)).astype(BF16)
    cs = _dot(lmat, lo) + _dot(lmat, mid) + _dot(lmat, hi) + carry_ref[0:1, :]
    carry_ref[...] = jnp.broadcast_to(cs[tt - 1:tt, :], carry_ref.shape)
    crow_ref[0] = cs.T[0:8, :]


def _gate_cumsum(small3, b_pad, grp):
    b, t, _ = small3.shape
    tt = _tile(t, 512)
    return pl.pallas_call(
        functools.partial(_gate_kernel, tt=tt, grp=grp),
        grid=(b, t // tt),
        in_specs=[pl.BlockSpec((1, tt, LANES), lambda i, j: (i, j, 0)),
                  pl.BlockSpec((1, LANES), lambda i, j: (0, 0))],
        out_specs=[pl.BlockSpec((1, tt, LANES), lambda i, j: (i, j, 0)),
                   pl.BlockSpec((1, 8, tt), lambda i, j: (i, 0, j))],
        out_shape=[jax.ShapeDtypeStruct((b, t, LANES), F32), jax.ShapeDtypeStruct((b, 8, t), F32)],
        scratch_shapes=[pltpu.VMEM((8, LANES), F32)],
        compiler_params=_cparams(("parallel", "arbitrary")),
        name="fox_gate_cumsum",
    )(small3, b_pad)


def _conv_kernel(glu_ref, st_ref, dw_ref, g_ref, b_ref, pw_ref, o_ref, so_ref, ext_ref, *, tm):
    j = pl.program_id(1)

    @pl.when(j == 0)
    def _():
        ext_ref[0:CONV_CARRY, :] = st_ref[0]

    glu = glu_ref[0]
    u = glu[:, :CONV_CH] * _sigmoid(glu[:, CONV_CH:])
    ext_ref[CONV_CARRY:CONV_CARRY + tm, :] = u
    off = CONV_CARRY - (CONV_WIDTH - 1)
    acc = jnp.zeros((tm, CONV_CH), F32)
    for k in range(CONV_WIDTH):
        acc = acc + dw_ref[k:k + 1, :] * ext_ref[pl.ds(off + k, tm), :]
    y = _layer_norm(acc, g_ref[...], b_ref[...])
    y = y * _sigmoid(y)
    o_ref[0] = _dot(y.astype(BF16), pw_ref[...])
    last = ext_ref[tm:tm + CONV_CARRY, :]
    so_ref[0] = last
    ext_ref[0:CONV_CARRY, :] = last


def _conv_module(glu3, state, dw, g, b, pw):
    bsz, t, _ = glu3.shape
    tm = _tile(t, 512)
    return pl.pallas_call(
        functools.partial(_conv_kernel, tm=tm),
        grid=(bsz, t // tm),
        in_specs=[pl.BlockSpec((1, tm, 2 * CONV_CH), lambda i, j: (i, j, 0)),
                  pl.BlockSpec((1, CONV_CARRY, CONV_CH), lambda i, j: (i, 0, 0)),
                  _resident(dw.shape), _resident(g.shape), _resident(b.shape), _resident(pw.shape)],
        out_specs=[pl.BlockSpec((1, tm, CONV_CH), lambda i, j: (i, j, 0)),
                   pl.BlockSpec((1, CONV_CARRY, CONV_CH), lambda i, j: (i, 0, 0))],
        out_shape=[jax.ShapeDtypeStruct((bsz, t, CONV_CH), F32),
                   jax.ShapeDtypeStruct((bsz, CONV_CARRY, CONV_CH), F32)],
        scratch_shapes=[pltpu.VMEM((tm + CONV_CARRY, CONV_CH), F32)],
        compiler_params=_cparams(("parallel", "arbitrary")),
        name="conv_module",
    )(glu3, state, dw, g, b, pw)


ROW_CHUNK = 32


def _flash_tile(s_ref, p_ref, m_ref, l_ref, acc_ref, v, bias_fn):
    rows, tk = s_ref.shape
    rc = min(ROW_CHUNK, rows)
    for i in range(rows // rc):
        r0 = i * rc
        sl = pl.ds(r0, rc)
        sc = bias_fn(s_ref[sl, :], r0)
        m_prev = m_ref[sl, :]
        m_new = jnp.maximum(m_prev, jnp.max(sc, axis=1, keepdims=True))
        alpha = jnp.exp(m_prev - m_new)
        p = jnp.exp(sc - jnp.tile(m_new, (1, tk // LANES)))
        l_ref[sl, :] = alpha * l_ref[sl, :] + jnp.sum(p, axis=1, keepdims=True)
        acc_ref[sl, :] = alpha * acc_ref[sl, :]
        m_ref[sl, :] = m_new
        p_ref[sl, :] = p.astype(BF16)
    acc_ref[...] += _dot(p_ref[...], v)


def _fox_kernel(q_ref, k_ref, v_ref, c_ref, o_ref, s_ref, p_ref, m_ref, l_ref, acc_ref, *, tq):
    qi = pl.program_id(2)
    _osm_init(m_ref, l_ref, acc_ref)
    q = q_ref[0]
    qs = jnp.concatenate([q[:, :LANES], q[:, LANES:]], axis=0)

    def tile(ki, diag):
        k0 = pl.multiple_of(ki * tq, tq)
        s_ref[...] = _nt(qs, k_ref[0, pl.ds(k0, tq), :])
        c2 = c_ref[0, 0, 0:2, pl.ds(k0, tq)]

        def bias(sc, r0):
            sc = sc - (c2[0:1, :] if r0 < tq else c2[1:2, :])
            if diag:
                col = lax.broadcasted_iota(jnp.int32, sc.shape, 1)
                row = lax.broadcasted_iota(jnp.int32, sc.shape, 0) + r0 % tq
                sc = jnp.where(col <= row, sc, -jnp.inf)
            return sc

        _flash_tile(s_ref, p_ref, m_ref, l_ref, acc_ref, v_ref[0, pl.ds(k0, tq), :], bias)

    def far(ki, carry):
        tile(ki, False)
        return carry

    lax.fori_loop(0, qi, far, 0)
    tile(qi, True)
    o = acc_ref[...] / l_ref[...]
    o_ref[0, :, :LANES] = o[:tq]
    o_ref[0, :, LANES:] = o[tq:]


def _fox_prompt(fq, fkb, fvb, cpair):
    b, t, _ = fq.shape
    tq = _tile(t, 512)
    rows = 2 * tq
    return pl.pallas_call(
        functools.partial(_fox_kernel, tq=tq),
        grid=(b, FOX_HEADS // 2, t // tq),
        in_specs=[pl.BlockSpec((1, tq, 2 * LANES), lambda i, p, j: (i, j, p)),
                  pl.BlockSpec((1, t, LANES), lambda i, p, j: (i, 0, p)),
                  pl.BlockSpec((1, t, LANES), lambda i, p, j: (i, 0, p)),
                  pl.BlockSpec((1, 1, 8, t), lambda i, p, j: (i, p, 0, 0))],
        out_specs=pl.BlockSpec((1, tq, 2 * LANES), lambda i, p, j: (i, j, p)),
        out_shape=jax.ShapeDtypeStruct((b, t, WIDE), F32),
        scratch_shapes=[pltpu.VMEM((rows, tq), F32), pltpu.VMEM((rows, tq), BF16),
                        pltpu.VMEM((rows, LANES), F32), pltpu.VMEM((rows, LANES), F32),
                        pltpu.VMEM((rows, LANES), F32)],
        compiler_params=_cparams(("parallel", "parallel", "arbitrary")),
        name="fox_prompt",
    )(fq, fkb, fvb, cpair)


def _bucket_thresholds():
    d = np.arange(4 * MAX_DISTANCE)
    max_exact = N_BUCKETS // 2
    df = np.maximum(d, 1).astype(np.float32)
    log_b = max_exact + (np.log(df / np.float32(max_exact)) / np.float32(math.log(MAX_DISTANCE / max_exact))
                         * np.float32(N_BUCKETS - max_exact)).astype(np.int32)
    bucket = np.where(d < max_exact, d, np.minimum(log_b, N_BUCKETS - 1))
    return [int(np.argmax(bucket >= bk)) for bk in range(N_BUCKETS)]


def _bias_kernel(rb_ref, o_ref, *, q0, a, b, maxd, sub_far, tqb, rows, thr):
    h = pl.program_id(0)
    blk = pl.program_id(1)
    nk = o_ref.shape[2]
    far = rb_ref[(N_BUCKETS - 1) * NSA_HEADS + h]

    def body(r, carry):
        i = lax.broadcasted_iota(jnp.int32, (rows, nk), 0) + (q0 + blk * tqb + r * rows)
        n = lax.broadcasted_iota(jnp.int32, (rows, nk), 1)
        dist = i - (a * n + b)
        val = jnp.full((rows, nk), rb_ref[h], F32)
        for bk in range(1, N_BUCKETS):
            val = jnp.where(dist >= thr[bk], rb_ref[bk * NSA_HEADS + h], val)
        if sub_far:
            val = val - far
        ok = dist >= 0
        if maxd is not None:
            ok = ok & (dist <= maxd)
        o_ref[0, pl.ds(pl.multiple_of(r * rows, rows), rows), :] = jnp.where(ok, val, NEG)
        return carry

    lax.fori_loop(0, tqb // rows, body, 0)


def _bias_table(rb_flat, nq, nk, *, q0, a, b, maxd=None, sub_far=False):
    tqb = _tile(nq, 128)
    rows = _tile(tqb, 32)
    return pl.pallas_call(
        functools.partial(_bias_kernel, q0=q0, a=a, b=b, maxd=maxd, sub_far=sub_far, tqb=tqb, rows=rows,
                          thr=_bucket_thresholds()),
        grid=(NSA_HEADS, nq // tqb),
        in_specs=[pl.BlockSpec(memory_space=pltpu.SMEM)],
        out_specs=pl.BlockSpec((1, tqb, nk), lambda h, i: (h, i, 0)),
        out_shape=jax.ShapeDtypeStruct((NSA_HEADS, nq, nk), F32),
        compiler_params=_cparams(("parallel", "arbitrary")),
        name="t5_bias_table",
    )(rb_flat)


def _compress_finish(lead, tail, tail_next, o_ref):
    nc = lead.shape[0]
    shifted = pltpu.roll(tail, nc - 1, 0)
    row = lax.broadcasted_iota(jnp.int32, lead.shape, 0)
    o_ref[0, 0] = (lead + jnp.where(row == nc - 1, tail_next, shifted)).astype(BF16)


def _compress_kernel(x_ref, nx_ref, w_ref, o_ref, *, nc):
    j = pl.program_id(2)
    lead = jnp.zeros((nc, LANES), F32)
    tail = jnp.zeros((nc, LANES), F32)
    tail_next = jnp.zeros((1, LANES), F32)
    for r in range(CMP_STRIDE):
        rows = x_ref[0, pl.ds(r, nc, stride=CMP_STRIDE), :]
        lead = lead + rows * w_ref[r:r + 1, :]
        tail = tail + rows * w_ref[CMP_STRIDE + r:CMP_STRIDE + r + 1, :]
        tail_next = tail_next + nx_ref[0, r:r + 1, :] * w_ref[CMP_STRIDE + r:CMP_STRIDE + r + 1, :]
    tail_next = jnp.where(j == pl.num_programs(2) - 1, 0.0, tail_next)
    _compress_finish(lead, tail, tail_next, o_ref)


def _compress(cmp3, w):
    b, t, _ = cmp3.shape
    tc = _tile(t, 2048)
    nc = tc // CMP_STRIDE
    nblk = t // tc
    sub = tc // CMP_STRIDE
    return pl.pallas_call(
        functools.partial(_compress_kernel, nc=nc),
        grid=(b, 2, nblk),
        in_specs=[pl.BlockSpec((1, tc, LANES), lambda i, h, j: (i, j, h)),
                  pl.BlockSpec((1, CMP_STRIDE, LANES),
                               lambda i, h, j: (i, jnp.minimum((j + 1) * sub, nblk * sub - 1), h)),
                  pl.BlockSpec((CMP_BLOCK, LANES), lambda i, h, j: (0, h))],
        out_specs=pl.BlockSpec((1, 1, nc, LANES), lambda i, h, j: (i, h, j, 0)),
        out_shape=jax.ShapeDtypeStruct((b, 2, t // CMP_STRIDE, LANES), BF16),
        compiler_params=_cparams(("parallel", "parallel", "arbitrary")),
        name="nsa_compress",
    )(cmp3, cmp3, w)


def _compress_paged_kernel(pt_ref, *refs, npg):
    pages = refs[:npg]
    nx_ref, w_ref, o_ref, x_ref = refs[npg:]
    j = pl.program_id(2)
    nc = npg * PAGE // CMP_STRIDE
    for jp, pg in enumerate(pages):
        x_ref[jp * PAGE:(jp + 1) * PAGE, :] = pg[0, 0].T
    nxt = nx_ref[0, 0].T
    lead = jnp.zeros((nc, LANES), F32)
    tail = jnp.zeros((nc, LANES), F32)
    tail_next = jnp.zeros((1, LANES), F32)
    for r in range(CMP_STRIDE):
        rows = x_ref[pl.ds(r, nc, stride=CMP_STRIDE), :]
        lead = lead + rows * w_ref[r:r + 1, :]
        tail = tail + rows * w_ref[CMP_STRIDE + r:CMP_STRIDE + r + 1, :]
        tail_next = tail_next + nxt[r:r + 1, :] * w_ref[CMP_STRIDE + r:CMP_STRIDE + r + 1, :]
    tail_next = jnp.where(j == pl.num_programs(2) - 1, 0.0, tail_next)
    _compress_finish(lead, tail, tail_next, o_ref)


def _page_map(b, i, pt, *, j, s, bb, layer, npg):
    return (pt[b * bb + s, i * npg + j], layer, 0, 0)


def _page_half_map(b, h, i, pt, *, j, layer, npg, last):
    return (pt[b, jnp.minimum(i * npg + j, last)], layer, h, 0)


def _compress_paged(cache4, page_table, layer, w):
    b, n_pages = page_table.shape
    npg = 2 * PAGES_PER_STEP if n_pages % (2 * PAGES_PER_STEP) == 0 else PAGES_PER_STEP
    nsteps = n_pages // npg
    nc = npg * PAGE // CMP_STRIDE
    pm = functools.partial(_page_half_map, layer=layer, npg=npg, last=n_pages - 1)
    in_specs = [pl.BlockSpec((1, 1, LANES, PAGE), functools.partial(pm, j=j)) for j in range(npg + 1)]
    in_specs.append(pl.BlockSpec((CMP_BLOCK, LANES), lambda i, h, j, pt: (0, h)))
    grid_spec = pltpu.PrefetchScalarGridSpec(
        num_scalar_prefetch=1, grid=(b, 2, nsteps), in_specs=in_specs,
        out_specs=pl.BlockSpec((1, 1, nc, LANES), lambda i, h, j, pt: (i, h, j, 0)),
        scratch_shapes=[pltpu.VMEM((npg * PAGE, LANES), F32)])
    return pl.pallas_call(
        functools.partial(_compress_paged_kernel, npg=npg),
        grid_spec=grid_spec,
        out_shape=jax.ShapeDtypeStruct((b, 2, n_pages * PAGE // CMP_STRIDE, LANES), BF16),
        compiler_params=_cparams(("parallel", "parallel", "arbitrary")),
        name="nsa_compress_paged",
    )(page_table, *([cache4] * (npg + 1)), w)


def _cmp_kernel(q_ref, kc_ref, vc_ref, tb_ref, a_ref, o_ref, sel_ref, s_ref, p_ref, pb_ref,
                *, tq, q_start, ns, topk, bb):
    for seq in range(bb):
        _cmp_one(q_ref.at[seq], kc_ref.at[seq, 0], vc_ref.at[seq, 0], tb_ref, a_ref, o_ref.at[seq],
                 sel_ref.at[seq, 0], s_ref.at[seq], p_ref.at[seq], pb_ref.at[seq],
                 tq=tq, q_start=q_start, ns=ns, topk=topk)


def _cmp_one(q_ref, kc_ref, vc_ref, tb_ref, a_ref, o_ref, sel_ref, s_ref, p_ref, pb_ref, *, tq, q_start, ns, topk):
    qb = pl.program_id(2)
    q = q_ref[...]
    q3 = jnp.concatenate([q[:, r * LANES:(r + 1) * LANES] for r in range(NSA_REP)], axis=0)
    s_ref[...] = _nt(q3, kc_ref[...])
    rc = min(ROW_CHUNK, tq)
    for i in range(NSA_REP * tq // rc):
        r0 = i * rc
        tb = tb_ref[r0 // tq, pl.ds(r0 % tq, rc), :]
        sc = s_ref[pl.ds(r0, rc), :] + tb
        m = jnp.max(sc, axis=1, keepdims=True)
        e = jnp.where(tb > NEG_TEST, jnp.exp(sc - m), 0.0)
        p = e / jnp.maximum(jnp.sum(e, axis=1, keepdims=True), 1e-30)
        p_ref[pl.ds(r0, rc), :] = p
        if rc % 16 == 0:
            pb_ref[pl.ds(r0, rc), :] = p.astype(BF16)
    if rc % 16 != 0:
        pb_ref[...] = p_ref[...].astype(BF16)
    o = _dot(pb_ref[...], vc_ref[...])
    for r in range(NSA_REP):
        o_ref[:, r * LANES:(r + 1) * LANES] = o[r * tq:(r + 1) * tq]
    imp_p = p_ref[0:tq, :] + p_ref[tq:2 * tq, :] + p_ref[2 * tq:3 * tq, :]
    nsp = a_ref.shape[0]
    tr = tq % LANES == 0
    hi = imp_p.astype(BF16)
    rem = imp_p - hi.astype(F32)
    mid = rem.astype(BF16)
    lo = (rem - mid.astype(F32)).astype(BF16)
    amat = a_ref[...]
    if tr:
        imp = _nt(amat, lo) + _nt(amat, mid) + _nt(amat, hi)
        shape, jdim, ax = (nsp, tq), 0, 0
    else:
        imp = _nt(lo, amat) + _nt(mid, amat) + _nt(hi, amat)
        shape, jdim, ax = (tq, nsp), 1, 1
    jj = lax.broadcasted_iota(jnp.int32, shape, jdim)
    t = lax.broadcasted_iota(jnp.int32, shape, 1 - jdim) + (q_start + qb * tq)
    cur = t // SEL_BLOCK
    forced = (jj == 0) | (jj == cur) | (jj == cur - 1)
    score = jnp.where(forced, 1e9, jnp.where(jj <= cur, imp, -1e9))
    score = jnp.where(jj < ns, score, -jnp.inf)
    jf = jj.astype(F32)
    chosen = jnp.zeros(shape, F32)
    for _ in range(topk):
        mx = jnp.max(score, axis=ax, keepdims=True)
        first = jnp.min(jnp.where(score == mx, jf, 1e9), axis=ax, keepdims=True)
        hit = jf == first
        chosen = jnp.where(hit, 1.0, chosen)
        score = jnp.where(hit, -jnp.inf, score)
    selb = jnp.where(chosen > 0.5, 0.0, SEL_NEG)
    sel_ref[...] = (selb.T if tr else selb).astype(BF16)


def _importance_matrix(ncp, nc, nsp):
    n = np.arange(ncp)[None, :]
    j = np.arange(nsp)[:, None]
    a = (n >= SEL_PER_CMP * j - 1) & (n <= SEL_PER_CMP * j + SEL_PER_CMP - 1) & (n < nc)
    return jnp.asarray(a, BF16)


def _nsa_cmp(nq, kvc, table, *, q_start, tk_total):
    b, t, _ = nq.shape
    ncp = kvc.shape[2]
    nc = -(-tk_total // CMP_STRIDE) - 1
    ns = -(-tk_total // SEL_BLOCK)
    nsp = -(-ns // LANES) * LANES
    tq = _tile(t, 128)
    bb = 2 * DEC_GROUP if (tq < LANES and b % (2 * DEC_GROUP) == 0) else 1
    amat = _importance_matrix(ncp, nc, nsp)
    g3 = NSA_REP * LANES
    rows = NSA_REP * tq
    return pl.pallas_call(
        functools.partial(_cmp_kernel, tq=tq, q_start=q_start, ns=ns, topk=min(SEL_TOPK, ns), bb=bb),
        grid=(b // bb, NSA_KV_HEADS, t // tq),
        in_specs=[pl.BlockSpec((bb, tq, g3), lambda i, g, j: (i, j, g)),
                  pl.BlockSpec((bb, 1, ncp, LANES), lambda i, g, j: (i, 0, 0, 0)),
                  pl.BlockSpec((bb, 1, ncp, LANES), lambda i, g, j: (i, 1, 0, 0)),
                  pl.BlockSpec((NSA_REP, tq, ncp), lambda i, g, j: (g, j, 0)),
                  _resident(amat.shape)],
        out_specs=[pl.BlockSpec((bb, tq, g3), lambda i, g, j: (i, j, g)),
                   pl.BlockSpec((bb, 1, tq, nsp), lambda i, g, j: (i, g, j, 0))],
        out_shape=[jax.ShapeDtypeStruct((b, t, WIDE), F32),
                   jax.ShapeDtypeStruct((b, NSA_KV_HEADS, t, nsp), BF16)],
        scratch_shapes=[pltpu.VMEM((bb, rows, ncp), F32), pltpu.VMEM((bb, rows, ncp), F32),
                        pltpu.VMEM((bb, rows, ncp), BF16)],
        compiler_params=_cparams(("parallel", "parallel", "arbitrary")),
        name="nsa_compressed_select",
    )(nq, kvc, kvc, table, amat)


def _sel_kernel(q_ref, sb_ref, k_ref, v_ref, oh_ref, nb_ref, o_ref,
                lhs_ref, s_ref, p_ref, m_ref, l_ref, acc_ref, *, tq):
    qi = pl.program_id(2)
    _osm_init(m_ref, l_ref, acc_ref)
    q = q_ref[0]
    lhs_ref[:, :LANES] = jnp.concatenate([q[:, r * LANES:(r + 1) * LANES] for r in range(NSA_REP)], axis=0)
    lhs_ref[:, LANES:] = jnp.concatenate([sb_ref[0, 0]] * NSA_REP, axis=0)

    def tile(ki, mode):
        k0 = pl.multiple_of(ki * tq, tq)
        kp = jnp.concatenate([k_ref[0, pl.ds(k0, tq), :], oh_ref[pl.ds(k0, tq), :]], axis=1)
        s_ref[...] = _nt(lhs_ref[...], kp)

        def bias(sc, r0):
            if mode == "far":
                return sc
            head = r0 // tq
            rows = pl.ds(r0 % tq, sc.shape[0])
            if mode == "near":
                return sc + nb_ref[head, rows, 0:tq]
            tb = nb_ref[head, rows, tq:2 * tq]
            return jnp.where(tb > NEG_TEST, sc + tb, -jnp.inf)

        _flash_tile(s_ref, p_ref, m_ref, l_ref, acc_ref, v_ref[0, pl.ds(k0, tq), :], bias)

    def far(ki, carry):
        tile(ki, "far")
        return carry

    lax.fori_loop(0, jnp.maximum(qi - 1, 0), far, 0)

    @pl.when(qi >= 1)
    def _():
        tile(qi - 1, "near")

    tile(qi, "diag")
    o = acc_ref[...] / l_ref[...]
    for r in range(NSA_REP):
        o_ref[0, :, r * LANES:(r + 1) * LANES] = o[r * tq:(r + 1) * tq]


def _nsa_sel_prompt(nq, selbias, selkv, onehot, near):
    b, t, _ = nq.shape
    tq = near.shape[1]
    g3 = NSA_REP * LANES
    rows = NSA_REP * tq
    return pl.pallas_call(
        functools.partial(_sel_kernel, tq=tq),
        grid=(b, NSA_KV_HEADS, t // tq),
        in_specs=[pl.BlockSpec((1, tq, g3), lambda i, g, j: (i, j, g)),
                  pl.BlockSpec((1, 1, tq, LANES), lambda i, g, j: (i, g, j, 0)),
                  pl.BlockSpec((1, t, LANES), lambda i, g, j: (i, 0, 0)),
                  pl.BlockSpec((1, t, LANES), lambda i, g, j: (i, 0, 1)),
                  _resident(onehot.shape),
                  pl.BlockSpec((NSA_REP, tq, 2 * tq), lambda i, g, j: (g, 0, 0))],
        out_specs=pl.BlockSpec((1, tq, g3), lambda i, g, j: (i, j, g)),
        out_shape=jax.ShapeDtypeStruct((b, t, WIDE), F32),
        scratch_shapes=[pltpu.VMEM((rows, 2 * LANES), BF16),
                        pltpu.VMEM((rows, tq), F32), pltpu.VMEM((rows, tq), BF16),
                        pltpu.VMEM((rows, LANES), F32), pltpu.VMEM((rows, LANES), F32),
                        pltpu.VMEM((rows, LANES), F32)],
        compiler_params=_cparams(("parallel", "parallel", "arbitrary")),
        name="nsa_selected_prompt",
    )(nq, selbias, selkv, selkv, onehot, near)


def _win_kernel(q_ref, *refs, nkb, tq, front):
    k_refs = refs[:nkb]
    v_refs = refs[nkb:2 * nkb]
    tb_ref, o_ref, s_ref, p_ref = refs[2 * nkb:]
    qb = pl.program_id(2)
    q = q_ref[0]
    q3 = jnp.concatenate([q[:, r * LANES:(r + 1) * LANES] for r in range(NSA_REP)], axis=0)
    k = jnp.concatenate([r[0] for r in k_refs], axis=0) if nkb > 1 else k_refs[0][0]
    v = jnp.concatenate([r[0] for r in v_refs], axis=0) if nkb > 1 else v_refs[0][0]
    nk = k.shape[0]
    s_ref[...] = _nt(q3, k)
    rc = min(ROW_CHUNK, tq)

    def softmax(mask_padding):
        for i in range(NSA_REP * tq // rc):
            r0 = i * rc
            sc = s_ref[pl.ds(r0, rc), :] + tb_ref[r0 // tq, pl.ds(r0 % tq, rc), :]
            if mask_padding:
                kl = lax.broadcasted_iota(jnp.int32, sc.shape, 1) + (qb * tq - front)
                sc = jnp.where(kl >= 0, sc, NEG)
            e = jnp.exp(sc - jnp.max(sc, axis=1, keepdims=True))
            p_ref[pl.ds(r0, rc), :] = e / jnp.sum(e, axis=1, keepdims=True)

    if front > 0:
        pl.when(qb * tq < front)(lambda: softmax(True))
        pl.when(qb * tq >= front)(lambda: softmax(False))
    else:
        softmax(False)
    o = _dot(p_ref[...].astype(BF16), v)
    for r in range(NSA_REP):
        o_ref[0, :, r * LANES:(r + 1) * LANES] = o[r * tq:(r + 1) * tq]


def _nsa_win(nq, winkv, table, *, tq, kblk, nkb, front):
    b, t, _ = nq.shape
    g3 = NSA_REP * LANES
    in_specs = [pl.BlockSpec((1, tq, g3), lambda i, g, j: (i, j, g))]
    for half in range(2):
        for jb in range(nkb):
            in_specs.append(pl.BlockSpec((1, kblk, LANES), functools.partial(
                lambda i, g, j, jb, half: (i, j + jb, half), jb=jb, half=half)))
    in_specs.append(pl.BlockSpec((NSA_REP, tq, nkb * kblk), lambda i, g, j: (g, 0, 0)))
    return pl.pallas_call(
        functools.partial(_win_kernel, nkb=nkb, tq=tq, front=front),
        grid=(b, NSA_KV_HEADS, t // tq),
        in_specs=in_specs,
        out_specs=pl.BlockSpec((1, tq, g3), lambda i, g, j: (i, j, g)),
        out_shape=jax.ShapeDtypeStruct((b, t, WIDE), F32),
        scratch_shapes=[pltpu.VMEM((NSA_REP * tq, nkb * kblk), F32), pltpu.VMEM((NSA_REP * tq, nkb * kblk), F32)],
        compiler_params=_cparams(("parallel", "parallel", "arbitrary")),
        name="nsa_window",
    )(nq, *([winkv] * (2 * nkb)), table)


def _out_kernel(oa_ref, of_ref, oc_ref, os_ref, ow_ref, sm_ref, x_ref, w_ref, g_ref, b_ref, o_ref, *, alpha):
    sg = _sigmoid(sm_ref[...])
    parts = [oa_ref[...].astype(BF16), of_ref[...].astype(BF16)]
    for h in range(NSA_HEADS):
        c0 = SM_NG + 3 * h
        lo, hi = h * LANES, (h + 1) * LANES
        o = (sg[:, c0:c0 + 1] * oc_ref[:, lo:hi] + sg[:, c0 + 1:c0 + 2] * os_ref[:, lo:hi]
             + sg[:, c0 + 2:c0 + 3] * ow_ref[:, lo:hi])
        parts.append(o.astype(BF16))
    mixed = jnp.concatenate(parts, axis=1)
    y = alpha * x_ref[...] + _dot(mixed, w_ref[...])
    o_ref[...] = _layer_norm(y, g_ref[...], b_ref[...])


def _out_ln(oa, of, oc, os_, ow, small, x2, w, g, b, alpha):
    m, d = x2.shape
    tm = _tile(m, 256)
    row = lambda width: pl.BlockSpec((tm, width), lambda i: (i, 0))
    return pl.pallas_call(
        functools.partial(_out_kernel, alpha=alpha),
        grid=(m // tm,),
        in_specs=[row(CONV_CH), row(WIDE), row(WIDE), row(WIDE), row(WIDE), row(LANES), row(d),
                  _resident(w.shape), _resident(g.shape), _resident(b.shape)],
        out_specs=row(d),
        out_shape=jax.ShapeDtypeStruct((m, d), F32),
        compiler_params=_cparams(("parallel",)),
        name="out_proj_ln",
    )(oa, of, oc, os_, ow, small, x2, w, g, b)


def _ffn_kernel(x_ref, st_ref, wi_ref, wo_ref, dw_ref, g_ref, b_ref, o_ref, so_ref, carry_ref, ext_ref,
                *, tm, dff, fc, alpha):
    j = pl.program_id(1)

    @pl.when(j == 0)
    def _():
        carry_ref[...] = st_ref[0]

    x = x_ref[0]
    xb = x.astype(BF16)
    acc = jnp.zeros(x.shape, F32)
    for c in range(dff // fc):
        lo, hi = c * fc, (c + 1) * fc
        up = _dot(xb, wi_ref[:, lo:hi])
        gate = _dot(xb, wi_ref[:, dff + lo:dff + hi])
        ext_ref[0:FFN_CARRY, :] = carry_ref[:, lo:hi]
        ext_ref[FFN_CARRY:FFN_CARRY + tm, :] = up
        y = (dw_ref[2:3, lo:hi] * up + dw_ref[1:2, lo:hi] * ext_ref[pl.ds(FFN_CARRY - 1, tm), :]
             + dw_ref[0:1, lo:hi] * ext_ref[pl.ds(FFN_CARRY - 2, tm), :])
        carry_ref[:, lo:hi] = ext_ref[tm:tm + FFN_CARRY, :]
        gelu = 0.5 * y * (1.0 + jnp.tanh(math.sqrt(2.0 / math.pi) * (y + 0.044715 * (y * y * y))))
        acc = acc + _dot((gelu * gate).astype(BF16), wo_ref[lo:hi, :])
    so_ref[0] = carry_ref[...]
    o_ref[0] = _layer_norm(alpha * x + acc, g_ref[...], b_ref[...])


def _ffn(x3, state, wi, wo, dw, g, b, alpha):
    bsz, t, d = x3.shape
    dff = wo.shape[0]
    tm = _tile(t, 512)
    fc = 256
    return pl.pallas_call(
        functools.partial(_ffn_kernel, tm=tm, dff=dff, fc=fc, alpha=alpha),
        grid=(bsz, t // tm),
        in_specs=[pl.BlockSpec((1, tm, d), lambda i, j: (i, j, 0)),
                  pl.BlockSpec((1, FFN_CARRY, dff), lambda i, j: (i, 0, 0)),
                  _resident(wi.shape), _resident(wo.shape), _resident(dw.shape),
                  _resident(g.shape), _resident(b.shape)],
        out_specs=[pl.BlockSpec((1, tm, d), lambda i, j: (i, j, 0)),
                   pl.BlockSpec((1, FFN_CARRY, dff), lambda i, j: (i, 0, 0))],
        out_shape=[jax.ShapeDtypeStruct((bsz, t, d), F32), jax.ShapeDtypeStruct((bsz, FFN_CARRY, dff), F32)],
        scratch_shapes=[pltpu.VMEM((FFN_CARRY, dff), F32), pltpu.VMEM((tm + FFN_CARRY, fc), F32)],
        compiler_params=_cparams(("parallel", "arbitrary"), vmem_mb=56),
        name="conv_ffn_ln",
    )(x3, state, wi, wo, dw, g, b)


DEC_GROUP = 2


def _fox_dec_kernel(pt_ref, q_ref, u_ref, kn_ref, vn_ref, cn_ref, *refs, npg, bb):
    n = bb * npg
    kp, vp, lp = refs[:n], refs[n:2 * n], refs[2 * n:3 * n]
    o_ref, m_ref, l_ref, acc_ref, carry_ref = refs[3 * n:]
    i = pl.program_id(1)

    @pl.when(i == 0)
    def _():
        _osm_init(m_ref, l_ref, acc_ref)
        carry_ref[...] = jnp.zeros(carry_ref.shape, F32)

    stats = [(m_ref.at[s], l_ref.at[s], acc_ref.at[s]) for s in range(bb)]
    scores, values = [], []
    for s in range(bb):
        lf = jnp.concatenate([r[0, 0] for r in lp[s * npg:(s + 1) * npg]], axis=0)
        cw = _dot3(lf, u_ref[...])
        carry = carry_ref[s]
        biases = []
        for j in range(npg):
            cj = cw[8 * j:8 * j + 8] + carry
            carry = carry + cw[8 * j:8 * j + 8, LANES - 1:LANES]
            biases.append(jnp.concatenate([cj] * 8, axis=0))
        carry_ref[s] = carry
        kt = jnp.concatenate([r[0, 0].astype(BF16) for r in kp[s * npg:(s + 1) * npg]], axis=1)
        scores.append(_dot(q_ref[s], kt) - jnp.concatenate(biases, axis=1))
    for s in range(bb):
        values.append(jnp.concatenate([r[0, 0].astype(BF16) for r in vp[s * npg:(s + 1) * npg]], axis=1))
    _osm_group(scores, values, stats, v_transposed=True)

    @pl.when(i == pl.num_programs(1) - 1)
    def _():
        scores = []
        for s in range(bb):
            bias = jnp.concatenate([carry_ref[s] + cn_ref[s]] * 8, axis=0)
            sn = _nt(q_ref[s], kn_ref[s].astype(BF16)) - bias
            col = lax.broadcasted_iota(jnp.int32, sn.shape, 1)
            qpos = lax.broadcasted_iota(jnp.int32, sn.shape, 0) // 8
            scores.append(jnp.where(col <= qpos, sn, -jnp.inf))
        _osm_group(scores, [vn_ref[s].astype(BF16) for s in range(bb)], stats)
        for s in range(bb):
            o_ref[s] = acc_ref[s] / l_ref[s][:, :1]


def _fox_decode(qbd, cache_k4, cache_v4, lf_t, page_table, layer, knew, vnew, cnew):
    b, n_pages = page_table.shape
    npg = PAGES_PER_STEP
    bb = DEC_GROUP if b % DEC_GROUP == 0 else 1
    umat = jnp.asarray(np.triu(np.ones((LANES, LANES), np.float32)), BF16)
    grp = lambda shape: pl.BlockSpec((bb,) + shape, lambda i, j, pt: (i, 0, 0))
    in_specs = [grp((64, FOX_W)), _resident(umat.shape), grp((PAGE, FOX_W)), grp((PAGE, FOX_W)), grp((8, LANES))]
    for shape in ((1, 1, FOX_W, PAGE), (1, 1, FOX_W, PAGE), (1, 1, 8, LANES)):
        in_specs += [pl.BlockSpec(shape, functools.partial(_page_map, j=j, s=s, bb=bb, layer=layer, npg=npg))
                     for s in range(bb) for j in range(npg)]
    grid_spec = pltpu.PrefetchScalarGridSpec(
        num_scalar_prefetch=1, grid=(b // bb, n_pages // npg), in_specs=in_specs,
        out_specs=grp((64, FOX_W)),
        scratch_shapes=[pltpu.VMEM((bb, 64, LANES), F32), pltpu.VMEM((bb, 64, LANES), F32),
                        pltpu.VMEM((bb, 64, FOX_W), F32), pltpu.VMEM((bb, 8, LANES), F32)])
    n = bb * npg
    return pl.pallas_call(
        functools.partial(_fox_dec_kernel, npg=npg, bb=bb),
        grid_spec=grid_spec,
        out_shape=jax.ShapeDtypeStruct((b, 64, FOX_W), F32),
        compiler_params=_cparams(("parallel", "arbitrary")),
        name="fox_decode",
    )(page_table, qbd, umat, knew, vnew, cnew, *([cache_k4] * n), *([cache_v4] * n), *([lf_t] * n))


def _sel_dec_kernel(pt_ref, q_ref, ss_ref, e_ref, tp_ref, tn_ref, sn_ref, kvn_ref, *refs, npg, bb):
    pages = refs[:bb * npg]
    o_ref, m_ref, l_ref, acc_ref = refs[bb * npg:]
    i = pl.program_id(1)
    last = i == pl.num_programs(1) - 1

    @pl.when(i == 0)
    def _():
        _osm_init(m_ref, l_ref, acc_ref)

    stats = [(m_ref.at[s], l_ref.at[s], acc_ref.at[s]) for s in range(bb)]
    near = jnp.where(last, tp_ref[...], 0.0)
    scores, values = [], []
    for s in range(bb):
        kt = jnp.concatenate([r[0, 0, :LANES, :].astype(BF16) for r in pages[s * npg:(s + 1) * npg]], axis=1)
        scores.append(_dot(q_ref[s], kt) + _dot(ss_ref[s, 0], e_ref[...]) + near)
    for s in range(bb):
        values.append(jnp.concatenate([r[0, 0, LANES:, :].astype(BF16)
                                       for r in pages[s * npg:(s + 1) * npg]], axis=1))
    _osm_group(scores, values, stats, v_transposed=True)

    @pl.when(last)
    def _():
        tn = tn_ref[...]
        scores = [jnp.where(tn > NEG_TEST, _nt(q_ref[s], kvn_ref[s][:, :LANES]) + tn + sn_ref[s], -jnp.inf)
                  for s in range(bb)]
        _osm_group(scores, [kvn_ref[s][:, LANES:] for s in range(bb)], stats)
        for s in range(bb):
            o_ref[s] = acc_ref[s] / l_ref[s]


def _nsa_sel_decode(q48, selstep, emat, tpast, tnew, selnew, kvnew, cache4, page_table, layer):
    b, n_pages = page_table.shape
    npg = PAGES_PER_STEP
    bb = 2 * DEC_GROUP if b % (2 * DEC_GROUP) == 0 else (DEC_GROUP if b % DEC_GROUP == 0 else 1)
    rows = NSA_HEADS * 8
    grp = lambda shape: pl.BlockSpec((bb,) + shape, lambda i, j, pt: (i, 0, 0))
    in_specs = [grp((rows, LANES)),
                pl.BlockSpec((bb, 1, rows, 2 * npg), lambda i, j, pt: (i, j, 0, 0)),
                _resident(emat.shape), _resident(tpast.shape), _resident(tnew.shape),
                grp((rows, LANES)), grp((PAGE, 2 * LANES))]
    in_specs += [pl.BlockSpec((1, 1, 2 * LANES, PAGE),
                              functools.partial(_page_map, j=j, s=s, bb=bb, layer=layer, npg=npg))
                 for s in range(bb) for j in range(npg)]
    grid_spec = pltpu.PrefetchScalarGridSpec(
        num_scalar_prefetch=1, grid=(b // bb, n_pages // npg), in_specs=in_specs,
        out_specs=grp((rows, LANES)),
        scratch_shapes=[pltpu.VMEM((bb, rows, LANES), F32), pltpu.VMEM((bb, rows, LANES), F32),
                        pltpu.VMEM((bb, rows, LANES), F32)])
    return pl.pallas_call(
        functools.partial(_sel_dec_kernel, npg=npg, bb=bb),
        grid_spec=grid_spec,
        out_shape=jax.ShapeDtypeStruct((b, rows, LANES), F32),
        compiler_params=_cparams(("parallel", "arbitrary")),
        name="nsa_selected_decode",
    )(page_table, q48, selstep, emat, tpast, tnew, selnew, kvnew, *([cache4] * (bb * npg)))


def _in_col_map():
    src = -np.ones(C_END, np.int64)
    scale = np.ones(C_END, np.float32)
    o_fq, o_fk, o_fv, o_ff = 2 * CONV_CH, 2 * CONV_CH + FOX_W, 2 * CONV_CH + 2 * FOX_W, 2 * CONV_CH + 3 * FOX_W
    o_nq = o_ff + FOX_HEADS
    o_nkv = o_nq + NSA_W
    o_ng = o_nkv + 6 * NSA_KV_W
    src[C_GLU:C_FQ] = np.arange(2 * CONV_CH)
    d = np.arange(HEAD_DIM)
    for h in range(FOX_HEADS):
        dst = C_FQ + LANES * h + HEAD_DIM * (h % 2)
        src[dst:dst + HEAD_DIM] = o_fq + HEAD_DIM * h + d
        scale[dst:dst + HEAD_DIM] = HEAD_DIM ** -0.5
    src[C_FK:C_FV] = o_fk + np.arange(FOX_W)
    src[C_FV:C_NQ] = o_fv + np.arange(FOX_W)
    for h in range(NSA_HEADS):
        dst = C_NQ + LANES * h + HEAD_DIM * (h // NSA_REP)
        src[dst:dst + HEAD_DIM] = o_nq + HEAD_DIM * h + d
        scale[dst:dst + HEAD_DIM] = HEAD_DIM ** -0.5
    src[C_CMP:C_SM] = o_nkv + np.arange(6 * NSA_KV_W)
    src[C_SM + SM_FF:C_SM + SM_FF + FOX_HEADS] = o_ff + np.arange(FOX_HEADS)
    src[C_SM + SM_NG:C_SM + SM_NG + 3 * NSA_HEADS] = o_ng + np.arange(3 * NSA_HEADS)
    return src, scale


def _out_row_map():
    n = CONV_CH + 2 * WIDE
    src = -np.ones(n, np.int64)
    src[:CONV_CH] = np.arange(CONV_CH)
    d = np.arange(HEAD_DIM)
    for h in range(FOX_HEADS):
        dst = CONV_CH + LANES * h + HEAD_DIM * (h % 2)
        src[dst:dst + HEAD_DIM] = CONV_CH + HEAD_DIM * h + d
    for h in range(NSA_HEADS):
        dst = CONV_CH + WIDE + LANES * h + HEAD_DIM * (h // NSA_REP)
        src[dst:dst + HEAD_DIM] = CONV_CH + FOX_W + HEAD_DIM * h + d
    return src


def _prep_layer(l, p):
    src, scale = _in_col_map()
    w_in = p['w_in'][l]
    wp = jnp.where(jnp.asarray(src >= 0)[None, :], jnp.take(w_in, jnp.asarray(np.maximum(src, 0)), axis=1), 0.0)
    wp = (wp * jnp.asarray(scale)[None, :]).astype(BF16)
    rsrc = _out_row_map()
    w_out = p['w_out'][l]
    wo = jnp.where(jnp.asarray(rsrc >= 0)[:, None], jnp.take(w_out, jnp.asarray(np.maximum(rsrc, 0)), axis=0), 0.0)
    dw = p['conv_dw'][l]
    fdw = p['ffn_dw'][l]
    return dict(
        wp=wp, wo=wo.astype(BF16),
        b_f=jnp.zeros((1, LANES), F32).at[0, SM_FF:SM_FF + FOX_HEADS].set(p['b_fgate'][l]),
        conv_dw=jnp.pad(dw, ((0, CONV_CARRY - dw.shape[0]), (0, 0))),
        conv_g=p['conv_ln_g'][l][None, :], conv_b=p['conv_ln_b'][l][None, :],
        conv_pw=p['conv_pw'][l].astype(BF16),
        cmp_w=jnp.transpose(p['nsa_cmp_w'][l], (1, 0, 2, 3)).reshape(CMP_BLOCK, 2 * NSA_KV_W),
        ln1_g=p['ln1_g'][l][None, :], ln1_b=p['ln1_b'][l][None, :],
        ffn_wi=p['ffn_w_in'][l].astype(BF16), ffn_wo=p['ffn_w_out'][l].astype(BF16),
        ffn_dw=jnp.pad(fdw, ((0, 8 - fdw.shape[0]), (0, 0))),
        ln2_g=p['ln2_g'][l][None, :], ln2_b=p['ln2_b'][l][None, :])


def _wide_from_heads(o, halves):
    z = jnp.zeros_like(o)
    sel = jnp.asarray(halves, jnp.int32)[:, None] == 0
    return jnp.concatenate([jnp.where(sel, o, z), jnp.where(sel, z, o)], axis=-1).reshape(o.shape[:-2] + (WIDE,))


def _layer_prompt(x, l, lw, tabs, alpha):
    b, t, d = x.shape
    m = b * t
    (glu, fq, fk, fv, fkb, fvb, nq, cmp_, sel, win, selb16, winb16, small) = _proj(x.reshape(m, d), lw['wp'])
    r3 = lambda a: a.reshape(b, t, a.shape[-1])
    out_a, conv_state = _conv_module(r3(glu), jnp.zeros((b, CONV_CARRY, CONV_CH), F32), lw['conv_dw'],
                                     lw['conv_g'], lw['conv_b'], lw['conv_pw'])
    lf, crow = _gate_cumsum(r3(small), lw['b_f'], t)
    cpair = jnp.pad(crow[:, :FOX_HEADS].reshape(b, FOX_HEADS // 2, 2, t), ((0, 0), (0, 0), (0, 6), (0, 0)))
    o_fox = _fox_prompt(r3(fq), r3(fkb), r3(fvb), cpair)
    kvc = _compress(r3(cmp_), lw['cmp_w'])
    nq3 = r3(nq)
    o_cmp, selbias = _nsa_cmp(nq3, kvc, tabs['cmp'], q_start=0, tk_total=t)
    o_sel = _nsa_sel_prompt(nq3, selbias, r3(selb16), tabs['onehot'], tabs['near'])
    tqw = tabs['win'].shape[1]
    winpad = jnp.pad(r3(winb16), ((0, 0), (WINDOW, 0), (0, 0)))
    o_win = _nsa_win(nq3, winpad, tabs['win'], tq=tqw, kblk=tqw, nkb=WINDOW // tqw + 1, front=WINDOW)
    x1 = _out_ln(out_a.reshape(m, CONV_CH), o_fox.reshape(m, WIDE), o_cmp.reshape(m, WIDE),
                 o_sel.reshape(m, WIDE), o_win.reshape(m, WIDE), small, x.reshape(m, d),
                 lw['wo'], lw['ln1_g'], lw['ln1_b'], alpha)
    dff = lw['ffn_wo'].shape[0]
    y, ffn_state = _ffn(x1.reshape(b, t, d), jnp.zeros((b, FFN_CARRY, dff), F32), lw['ffn_wi'], lw['ffn_wo'],
                        lw['ffn_dw'], lw['ln2_g'], lw['ln2_b'], alpha)
    buf = min(WINDOW, t)
    new = dict(fox_k=fk.reshape(b, t, FOX_HEADS, HEAD_DIM), fox_v=fv.reshape(b, t, FOX_HEADS, HEAD_DIM),
               fox_logf=lf[:, :, :FOX_HEADS],
               nsa_cmp=cmp_.reshape(b, t, 2, NSA_KV_HEADS, HEAD_DIM),
               nsa_sel=sel.reshape(b, t, 2, NSA_KV_HEADS, HEAD_DIM),
               nsa_win=r3(win)[:, t - buf:].reshape(b, buf, 2, NSA_KV_HEADS, HEAD_DIM),
               conv=conv_state[:, CONV_CARRY - (CONV_WIDTH - 1):],
               ffn_conv=ffn_state[:, FFN_CARRY - (FFN_CONV_WIDTH - 1):])
    return y, new


def _layer_sample(x, l, lw, tabs, alpha, cache):
    b, t, d = x.shape
    m = b * t
    pt = cache['page_table']
    n_pages = pt.shape[1]
    past = n_pages * PAGE
    (glu, fq, fk, fv, fkb, fvb, nq, cmp_, sel, win, selb16, winb16, small) = _proj(x.reshape(m, d), lw['wp'])
    r3 = lambda a: a.reshape(b, t, a.shape[-1])
    st = jnp.pad(cache['conv'][l], ((0, 0), (CONV_CARRY - (CONV_WIDTH - 1), 0), (0, 0)))
    out_a, conv_state = _conv_module(r3(glu), st, lw['conv_dw'], lw['conv_g'], lw['conv_b'], lw['conv_pw'])
    lf, crow = _gate_cumsum(small.reshape(1, m, LANES), lw['b_f'], t)
    lf = lf.reshape(b, t, LANES)
    cnew = jnp.pad(jnp.transpose(crow[0].reshape(8, b, t), (1, 0, 2)), ((0, 0), (0, 0), (0, LANES - t)))
    q6 = r3(fq).reshape(b, t, FOX_HEADS, 2, HEAD_DIM)
    qc = jnp.where((jnp.arange(FOX_HEADS) % 2 == 0)[:, None], q6[:, :, :, 0], q6[:, :, :, 1])
    eye = jnp.eye(8, FOX_HEADS, dtype=qc.dtype)
    qbd = (jnp.pad(qc, ((0, 0), (0, 0), (0, 2), (0, 0)))[:, :, :, None, :] * eye[None, None, :, :, None])
    qbd = qbd.reshape(b, t * 8, FOX_W)
    padrows = lambda a: jnp.pad(a, ((0, 0), (0, PAGE - t), (0, 0)))
    o64 = _fox_decode(qbd, cache['fox_k4'], cache['fox_v4'], cache['lf_t'], pt, l,
                      padrows(r3(fk)), padrows(r3(fv)), cnew)
    o5 = o64.reshape(b, t, 8, FOX_HEADS, HEAD_DIM)[:, :, :FOX_HEADS]
    o_f = jnp.moveaxis(jnp.diagonal(o5, axis1=2, axis2=3), -1, 2)
    o_fox = _wide_from_heads(o_f, [h % 2 for h in range(FOX_HEADS)])
    kvc = _compress_paged(cache['cmp4'], pt, l, lw['cmp_w'])
    nq3 = r3(nq)
    o_cmp, selbias = _nsa_cmp(nq3, kvc, tabs['cmp'], q_start=past, tk_total=past + t)
    npg = PAGES_PER_STEP
    nsteps = n_pages // npg
    sb = selbias[:, :, :, :2 * n_pages].reshape(b, NSA_KV_HEADS, 1, t, nsteps, 2 * npg)
    sb = jnp.broadcast_to(sb, (b, NSA_KV_HEADS, NSA_REP, t, nsteps, 2 * npg))
    selstep = jnp.transpose(sb, (0, 4, 1, 2, 3, 5)).reshape(b, nsteps, NSA_HEADS * t, 2 * npg)
    sn = selbias[:, :, :, 2 * n_pages].astype(F32)
    selnew = jnp.broadcast_to(sn[:, :, None, :, None], (b, NSA_KV_HEADS, NSA_REP, t, LANES))
    selnew = selnew.reshape(b, NSA_HEADS * t, LANES)
    q48 = jnp.transpose(nq3.reshape(b, t, NSA_HEADS, LANES), (0, 2, 1, 3)).reshape(b, NSA_HEADS * t, LANES)
    o48 = _nsa_sel_decode(q48, selstep, tabs['emat'], tabs['sel_past'], tabs['sel_new'], selnew,
                          padrows(r3(selb16)), cache['sel4'], pt, l)
    o_sel = jnp.transpose(o48.reshape(b, NSA_HEADS, t, LANES), (0, 2, 1, 3)).reshape(b, t, WIDE)
    win_all = jnp.concatenate([cache['nsa_win'][l].reshape(b, -1, 2 * NSA_KV_W), r3(win)], axis=1)
    wrows = tabs['win'].shape[2]
    winpad = jnp.pad(win_all.astype(BF16), ((0, 0), (0, wrows - win_all.shape[1]), (0, 0)))
    o_win = _nsa_win(nq3, winpad, tabs['win'], tq=t, kblk=wrows, nkb=1, front=0)
    x1 = _out_ln(out_a.reshape(m, CONV_CH), o_fox.reshape(m, WIDE), o_cmp.reshape(m, WIDE),
                 o_sel.reshape(m, WIDE), o_win.reshape(m, WIDE), small, x.reshape(m, d),
                 lw['wo'], lw['ln1_g'], lw['ln1_b'], alpha)
    fst = jnp.pad(cache['ffn_conv'][l], ((0, 0), (FFN_CARRY - (FFN_CONV_WIDTH - 1), 0), (0, 0)))
    y, ffn_state = _ffn(x1.reshape(b, t, d), fst, lw['ffn_wi'], lw['ffn_wo'], lw['ffn_dw'],
                        lw['ln2_g'], lw['ln2_b'], alpha)
    wlen = cache['nsa_win'].shape[2]
    new = dict(fox_k=fk.reshape(b, t, FOX_HEADS, HEAD_DIM), fox_v=fv.reshape(b, t, FOX_HEADS, HEAD_DIM),
               fox_logf=lf[:, :, :FOX_HEADS],
               nsa_cmp=cmp_.reshape(b, t, 2, NSA_KV_HEADS, HEAD_DIM),
               nsa_sel=sel.reshape(b, t, 2, NSA_KV_HEADS, HEAD_DIM),
               nsa_win=win_all[:, win_all.shape[1] - wlen:].reshape(b, wlen, 2, NSA_KV_HEADS, HEAD_DIM),
               conv=conv_state[:, CONV_CARRY - (CONV_WIDTH - 1):],
               ffn_conv=ffn_state[:, FFN_CARRY - (FFN_CONV_WIDTH - 1):])
    return y, new


ROW_KEYS = ('fox_k', 'fox_v', 'fox_logf', 'nsa_cmp', 'nsa_sel')
STATE_KEYS = ('nsa_win', 'conv', 'ffn_conv')


def _collect(news):
    out = {k: jnp.stack([n[k] for n in news], axis=1) for k in ROW_KEYS}
    out.update({k: jnp.stack([n[k] for n in news], axis=0) for k in STATE_KEYS})
    return out


def kernel(x_prompt, x_sample, cache_fox_k, cache_fox_v, cache_fox_logf, cache_nsa_cmp, cache_nsa_sel, state_nsa_win, state_conv, state_ffn_conv, page_table, w_in, b_fgate, conv_dw, conv_ln_g, conv_ln_b, conv_pw, nsa_cmp_w, rel_bias, w_out, ln1_g, ln1_b, ffn_w_in, ffn_dw, ffn_w_out, ln2_g, ln2_b):
    p = dict(w_in=w_in, b_fgate=b_fgate, conv_dw=conv_dw, conv_ln_g=conv_ln_g, conv_ln_b=conv_ln_b,
             conv_pw=conv_pw, nsa_cmp_w=nsa_cmp_w, rel_bias=rel_bias, w_out=w_out, ln1_g=ln1_g, ln1_b=ln1_b,
             ffn_w_in=ffn_w_in, ffn_dw=ffn_dw, ffn_w_out=ffn_w_out, ln2_g=ln2_g, ln2_b=ln2_b)
    depth = w_in.shape[0]
    alpha = (2 * depth) ** 0.25
    lws = [_prep_layer(l, p) for l in range(depth)]
    rb_flat = rel_bias.reshape(-1)

    b, t, _ = x_prompt.shape
    tq_sel = _tile(t, 512)
    tq_win = _tile(t, 256)
    tabs = dict(
        cmp=_bias_table(rb_flat, t, t // CMP_STRIDE, q0=0, a=CMP_STRIDE, b=CMP_BLOCK - 1),
        near=_bias_table(rb_flat, tq_sel, 2 * tq_sel, q0=0, a=1, b=-tq_sel, sub_far=True),
        win=_bias_table(rb_flat, tq_win, WINDOW + tq_win, q0=0, a=1, b=-WINDOW, maxd=WINDOW),
        onehot=jnp.asarray((np.arange(t)[:, None] // SEL_BLOCK) == np.arange(LANES)[None, :], BF16))
    x = x_prompt
    news = []
    for l in range(depth):
        x, new = _layer_prompt(x, l, lws[l], tabs, alpha)
        news.append(new)
    y_prompt, sp = x, _collect(news)

    db, dt, _ = x_sample.shape
    n_pool = cache_fox_k.shape[0]
    n_pages = page_table.shape[1]
    past = n_pages * PAGE
    npg = PAGES_PER_STEP
    wrows = -(-(state_nsa_win.shape[2] + dt) // LANES) * LANES
    near_s = _bias_table(rb_flat, dt, 2 * LANES, q0=0, a=1, b=-LANES, sub_far=True).reshape(NSA_HEADS * dt, 2 * LANES)
    tabs_s = dict(
        cmp=_bias_table(rb_flat, dt, past // CMP_STRIDE, q0=past, a=CMP_STRIDE, b=CMP_BLOCK - 1),
        win=_bias_table(rb_flat, dt, wrows, q0=state_nsa_win.shape[2], a=1, b=0, maxd=WINDOW),
        sel_past=jnp.pad(near_s[:, :LANES], ((0, 0), ((npg - 1) * PAGE, 0))),
        sel_new=near_s[:, LANES:],
        emat=jnp.asarray((np.arange(npg * PAGE)[None, :] // SEL_BLOCK) == np.arange(2 * npg)[:, None], BF16))
    lf_t = jnp.pad(jnp.swapaxes(cache_fox_logf, 2, 3), ((0, 0), (0, 0), (0, 8 - FOX_HEADS), (0, 0)))
    pos_minor = lambda c: jnp.moveaxis(c, 2, -1).reshape(n_pool, depth, -1, PAGE)
    cache = dict(
        page_table=page_table,
        fox_k4=pos_minor(cache_fox_k), fox_v4=pos_minor(cache_fox_v), lf_t=lf_t,
        cmp4=pos_minor(cache_nsa_cmp), sel4=pos_minor(cache_nsa_sel),
        nsa_win=state_nsa_win, conv=state_conv, ffn_conv=state_ffn_conv)
    x = x_sample
    news = []
    for l in range(depth):
        x, new = _layer_sample(x, l, lws[l], tabs_s, alpha, cache)
        news.append(new)
    y_sample, ss = x, _collect(news)

    return (y_prompt, y_sample,
            sp['fox_k'], sp['fox_v'], sp['fox_logf'], sp['nsa_cmp'], sp['nsa_sel'], sp['nsa_win'], sp['conv'], sp['ffn_conv'],
            ss['fox_k'], ss['fox_v'], ss['fox_logf'], ss['nsa_cmp'], ss['nsa_sel'], ss['nsa_win'], ss['conv'], ss['ffn_conv'])
```
